```python
import jax
import jax.numpy as jnp
from jax import lax
import numpy as np

D_MODEL = 2048
BATCH = 2
SEQ = 4096
DEPTH = 4

N_MIXERS = 3
CHUNK = 128
EPS = 1e-6

RET_HEADS = 8
RET_DK = D_MODEL // RET_HEADS
RET_DV = 2 * D_MODEL // RET_HEADS
RET_IN = 2 * RET_HEADS * RET_DK + 2 * RET_HEADS * RET_DV
ROPE_BASE = 10000.0

GMLP_DFFN = 6 * D_MODEL
GMLP_HALF = GMLP_DFFN // 2
GMLP_GROUPS = 8
GMLP_GW = GMLP_HALF // GMLP_GROUPS

FOX_HEADS = 16
FOX_HD = D_MODEL // FOX_HEADS
FOX_IN = 4 * FOX_HEADS * FOX_HD + FOX_HEADS

FFN_HIDDEN = -(-8 * D_MODEL // (3 * 256)) * 256

N_RET = (DEPTH + N_MIXERS - 1) // N_MIXERS
N_GMLP = (DEPTH + N_MIXERS - 2) // N_MIXERS
N_FOX = DEPTH // N_MIXERS

kernel_name = "hybrid_ret_gmlp_fox_trunk"


def rms_norm(x, g):
    xf = x.astype(jnp.float32)
    y = xf * lax.rsqrt(jnp.mean(xf * xf, axis=-1, keepdims=True) + EPS)
    return (y * g.astype(jnp.float32)).astype(x.dtype)


def standardize(x):
    xf = x.astype(jnp.float32)
    xc = xf - jnp.mean(xf, axis=-1, keepdims=True)
    return xc * lax.rsqrt(jnp.mean(xc * xc, axis=-1, keepdims=True) + EPS)


def rotary(t, positions):
    half = t.shape[-1] // 2
    inv_freq = ROPE_BASE ** (-jnp.arange(half, dtype=jnp.float32) / half)
    ang = positions.astype(jnp.float32)[:, :, None, None] * inv_freq
    cos, sin = jnp.cos(ang), jnp.sin(ang)
    t1, t2 = t[..., :half], t[..., half:]
    return jnp.concatenate([t1 * cos - t2 * sin, t2 * cos + t1 * sin], axis=-1)


def retention_mixer(h, positions, w_in, gn_g, w_out):
    f32 = jnp.float32
    B, S, _ = h.shape
    H, dk, dv, C = RET_HEADS, RET_DK, RET_DV, CHUNK
    nc = S // C
    proj = h @ w_in
    q, k, v, g = jnp.split(proj, [H * dk, 2 * H * dk, 2 * H * dk + H * dv], axis=-1)
    q = rotary(q.astype(f32).reshape(B, S, H, dk), positions)
    k = rotary(k.astype(f32).reshape(B, S, H, dk), positions) * (dk ** -0.5)
    v = v.astype(f32).reshape(B, S, H, dv)

    def to_chunks(t):
        return t.reshape(B, nc, C, H, t.shape[-1]).transpose(1, 0, 3, 2, 4)

    log_gamma = jnp.log1p(-jnp.exp2(-5.0 - jnp.arange(H, dtype=f32)))
    idx = jnp.arange(C, dtype=f32)
    dist = idx[:, None] - idx[None, :]
    decay_in = jnp.where(dist >= 0,
                         jnp.exp(jnp.maximum(dist, 0.0)[None] * log_gamma[:, None, None]),
                         0.0)
    q_decay = jnp.exp((idx + 1.0)[None, :] * log_gamma[:, None])[..., None]
    k_decay = jnp.exp((C - 1.0 - idx)[None, :] * log_gamma[:, None])[..., None]
    chunk_decay = jnp.exp(C * log_gamma)[:, None, None]

    def step(state, xs):
        qi, ki, vi = xs
        scores = jnp.einsum('bhnd,bhmd->bhnm', qi, ki) * decay_in
        inner = jnp.einsum('bhnm,bhme->bhne', scores, vi)
        cross = jnp.einsum('bhnd,bhde->bhne', qi, state) * q_decay
        state = state * chunk_decay + jnp.einsum('bhmd,bhme->bhde', ki * k_decay, vi)
        return state, inner + cross

    state0 = jnp.zeros((B, H, dk, dv), f32)
    _, o = lax.scan(step, state0, (to_chunks(q), to_chunks(k), to_chunks(v)))
    o = o.transpose(1, 0, 3, 2, 4).reshape(B, S, H, dv)
    o = standardize(o).reshape(B, S, H * dv) * gn_g.astype(f32)
    y = jax.nn.silu(g.astype(f32)) * o
    return y.astype(h.dtype) @ w_out


def gmlp_mixer(h, w_in, ln_g, ln_b, w_s, b_s, w_out):
    f32 = jnp.float32
    B, S, _ = h.shape
    G, gw, C = GMLP_GROUPS, GMLP_GW, CHUNK
    nc = S // C
    z = jax.nn.gelu(h @ w_in, approximate=False)
    u, v = jnp.split(z, 2, axis=-1)
    v = (standardize(v) * ln_g.astype(f32) + ln_b.astype(f32)).astype(h.dtype)
    v = v.reshape(B, nc, C, G, gw)
    w_causal = jnp.tril(w_s)
    mixed = jnp.einsum('gts,bnsgc->bntgc', w_causal, v) + b_s.T[None, None, :, :, None]
    y = u * mixed.reshape(B, S, GMLP_HALF)
    return y @ w_out


def fox_mixer(h, w_in, b_f, qn_g, kn_g, w_out):
    f32 = jnp.float32
    B, S, _ = h.shape
    H, hd, C = FOX_HEADS, FOX_HD, CHUNK
    D = H * hd
    nb = S // C
    proj = h @ w_in
    q, k, v, g, f_logit = jnp.split(proj, [D, 2 * D, 3 * D, 4 * D], axis=-1)
    q = rms_norm(q.reshape(B, S, H, hd), qn_g).astype(f32)
    k = rms_norm(k.reshape(B, S, H, hd), kn_g).astype(f32)
    v = v.reshape(B, S, H, hd).astype(f32)
    log_f = jax.nn.log_sigmoid(f_logit.astype(f32) + b_f.astype(f32))
    c = jnp.cumsum(log_f, axis=1)
    scale = hd ** -0.5
    q_blocks = q.reshape(B, nb, C, H, hd).transpose(1, 0, 3, 2, 4)
    c_blocks = c.reshape(B, nb, C, H).transpose(1, 0, 3, 2)
    k_all = k.transpose(0, 2, 1, 3)
    v_all = v.transpose(0, 2, 1, 3)
    c_keys = c.transpose(0, 2, 1)
    key_pos = jnp.arange(S)

    def attend_block(args):
        qi, ci, bi = args
        q_pos = bi * C + jnp.arange(C)
        logits = (jnp.einsum('bhqd,bhkd->bhqk', qi, k_all) * scale
                  + (ci[..., :, None] - c_keys[:, :, None, :]))
        logits = jnp.where(key_pos[None, :] <= q_pos[:, None], logits, -jnp.inf)
        p = jax.nn.softmax(logits, axis=-1)
        return jnp.einsum('bhqk,bhkd->bhqd', p, v_all)

    o = lax.map(attend_block, (q_blocks, c_blocks, jnp.arange(nb)))
    o = o.transpose(1, 0, 3, 2, 4).reshape(B, S, D)
    y = jax.nn.sigmoid(g.astype(f32)) * o
    return y.astype(h.dtype) @ w_out


def swiglu(h, w_gate, w_up, w_down):
    return (jax.nn.silu(h @ w_gate) * (h @ w_up)) @ w_down


def _normal(key, shape, scale):
    return jax.random.normal(key, shape, jnp.float32) * scale


def setup_inputs(seed: int = 0) -> dict:
    key = jax.random.key(seed)
    ks = jax.random.split(key, 24)
    D = D_MODEL
    out_scale = (2.0 * DEPTH) ** -0.5
    return {
        "x": _normal(ks[0], (BATCH, SEQ, D), 1.0),
        "positions": jnp.broadcast_to(jnp.arange(SEQ, dtype=jnp.int32), (BATCH, SEQ)),
        "mix_norm_g": 1.0 + _normal(ks[1], (DEPTH, D), 0.1),
        "ffn_norm_g": 1.0 + _normal(ks[2], (DEPTH, D), 0.1),
        "ret_w_in": _normal(ks[3], (N_RET, D, RET_IN), D ** -0.5),
        "ret_gn_g": 1.0 + _normal(ks[4], (N_RET, RET_HEADS * RET_DV), 0.1),
        "ret_w_out": _normal(ks[5], (N_RET, RET_HEADS * RET_DV, D), (RET_HEADS * RET_DV) ** -0.5 * out_scale),
        "gmlp_w_in": _normal(ks[6], (N_GMLP, D, GMLP_DFFN), D ** -0.5),
        "gmlp_ln_g": 1.0 + _normal(ks[7], (N_GMLP, GMLP_HALF), 0.1),
        "gmlp_ln_b": _normal(ks[8], (N_GMLP, GMLP_HALF), 0.1),
        "gmlp_w_s": _normal(ks[9], (N_GMLP, GMLP_GROUPS, CHUNK, CHUNK), 0.5 * CHUNK ** -0.5),
        "gmlp_b_s": 1.0 + _normal(ks[10], (N_GMLP, GMLP_GROUPS, CHUNK), 0.1),
        "gmlp_w_out": _normal(ks[11], (N_GMLP, GMLP_HALF, D), GMLP_HALF ** -0.5 * out_scale),
        "fox_w_in": _normal(ks[12], (N_FOX, D, FOX_IN), D ** -0.5),
        "fox_b_f": jax.random.uniform(ks[13], (N_FOX, FOX_HEADS), jnp.float32, 1.0, 5.0),
        "fox_qn_g": 1.0 + _normal(ks[14], (N_FOX, FOX_HD), 0.1),
        "fox_kn_g": 1.0 + _normal(ks[15], (N_FOX, FOX_HD), 0.1),
        "fox_w_out": _normal(ks[16], (N_FOX, D, D), D ** -0.5 * out_scale),
        "ffn_w_gate": _normal(ks[17], (DEPTH, D, FFN_HIDDEN), D ** -0.5),
        "ffn_w_up": _normal(ks[18], (DEPTH, D, FFN_HIDDEN), D ** -0.5),
        "ffn_w_down": _normal(ks[19], (DEPTH, FFN_HIDDEN, D), FFN_HIDDEN ** -0.5 * out_scale),
    }


def reference(x, positions, mix_norm_g, ffn_norm_g,
              ret_w_in, ret_gn_g, ret_w_out,
              gmlp_w_in, gmlp_ln_g, gmlp_ln_b, gmlp_w_s, gmlp_b_s, gmlp_w_out,
              fox_w_in, fox_b_f, fox_qn_g, fox_kn_g, fox_w_out,
              ffn_w_gate, ffn_w_up, ffn_w_down):
    for i in range(DEPTH):
        kind, j = i % N_MIXERS, i // N_MIXERS
        h = rms_norm(x, mix_norm_g[i])
        if kind == 0:
            y = retention_mixer(h, positions, ret_w_in[j], ret_gn_g[j], ret_w_out[j])
        elif kind == 1:
            y = gmlp_mixer(h, gmlp_w_in[j], gmlp_ln_g[j], gmlp_ln_b[j],
                           gmlp_w_s[j], gmlp_b_s[j], gmlp_w_out[j])
        else:
            y = fox_mixer(h, fox_w_in[j], fox_b_f[j], fox_qn_g[j], fox_kn_g[j], fox_w_out[j])
        x = x + y
        h = rms_norm(x, ffn_norm_g[i])
        x = x + swiglu(h, ffn_w_gate[i], ffn_w_up[i], ffn_w_down[i])
    return x
```

```python
import functools

import jax
import jax.numpy as jnp
from jax import lax
from jax.experimental import pallas as pl
from jax.experimental.pallas import tpu as pltpu

F32 = jnp.float32
BF16 = jnp.bfloat16

EPS = 1e-6
CHUNK = 128
ROPE_BASE = 10000.0
RET_HEADS = 8
GMLP_GROUPS = 8
FOX_HEADS = 16
N_MIXERS = 3

LANES = 128
VMEM_LIMIT = 56 * 1024 * 1024


def _params(n_axes):
    return pltpu.CompilerParams(
        dimension_semantics=("arbitrary",) * n_axes, vmem_limit_bytes=VMEM_LIMIT)


def _rmsnorm_kernel(x_ref, g_ref, o_ref):
    x = x_ref[...]
    ms = jnp.mean(x * x, axis=-1, keepdims=True)
    o_ref[...] = (x * lax.rsqrt(ms + EPS) * g_ref[...]).astype(o_ref.dtype)


def _rmsnorm(x, g_all, layer, tm=512):
    T, D = x.shape
    return pl.pallas_call(
        _rmsnorm_kernel,
        grid=(T // tm,),
        in_specs=[pl.BlockSpec((tm, D), lambda i: (i, 0)),
                  pl.BlockSpec((None, 1, D), lambda i: (layer, 0, 0))],
        out_specs=pl.BlockSpec((tm, D), lambda i: (i, 0)),
        out_shape=jax.ShapeDtypeStruct((T, D), BF16),
        compiler_params=_params(1),
        name="rmsnorm",
    )(x, g_all)


def _gelu_exact(x):
    return 0.5 * x * (1.0 + lax.erf(x * (2.0 ** -0.5)))


def _mm_kernel(x_ref, w_ref, o_ref, wb_ref, *, act):
    @pl.when(pl.program_id(1) == 0)
    def _():
        wb_ref[...] = w_ref[...].astype(BF16)

    acc = jnp.dot(x_ref[...], wb_ref[...], preferred_element_type=F32)
    if act == "gelu":
        acc = _gelu_exact(acc)
    o_ref[...] = acc.astype(o_ref.dtype)


def _mm(x, w_all, layer, n_out, *, act=None, out_dtype=BF16, tm=1024, tn=1024, name="mm"):
    T, K = x.shape
    return pl.pallas_call(
        functools.partial(_mm_kernel, act=act),
        grid=(n_out // tn, T // tm),
        in_specs=[pl.BlockSpec((tm, K), lambda j, i: (i, 0)),
                  pl.BlockSpec((None, K, tn), lambda j, i: (layer, 0, j))],
        out_specs=pl.BlockSpec((tm, tn), lambda j, i: (i, j)),
        out_shape=jax.ShapeDtypeStruct((T, n_out), out_dtype),
        scratch_shapes=[pltpu.VMEM((K, tn), BF16)],
        compiler_params=_params(2),
        name=name,
    )(x, w_all)


def _mm_swiglu_kernel(x_ref, wg_ref, wu_ref, o_ref, wgb_ref, wub_ref):
    @pl.when(pl.program_id(1) == 0)
    def _():
        wgb_ref[...] = wg_ref[...].astype(BF16)
        wub_ref[...] = wu_ref[...].astype(BF16)

    x = x_ref[...]
    g = jnp.dot(x, wgb_ref[...], preferred_element_type=F32)
    u = jnp.dot(x, wub_ref[...], preferred_element_type=F32)
    o_ref[...] = (jax.nn.silu(g) * u).astype(o_ref.dtype)


def _mm_swiglu(x, wg_all, wu_all, layer, *, tm=1024, tn=512):
    T, K = x.shape
    F = wg_all.shape[-1]
    wspec = pl.BlockSpec((None, K, tn), lambda j, i: (layer, 0, j))
    return pl.pallas_call(
        _mm_swiglu_kernel,
        grid=(F // tn, T // tm),
        in_specs=[pl.BlockSpec((tm, K), lambda j, i: (i, 0)), wspec, wspec],
        out_specs=pl.BlockSpec((tm, tn), lambda j, i: (i, j)),
        out_shape=jax.ShapeDtypeStruct((T, F), BF16),
        scratch_shapes=[pltpu.VMEM((K, tn), BF16), pltpu.VMEM((K, tn), BF16)],
        compiler_params=_params(2),
        name="mm_swiglu",
    )(x, wg_all, wu_all)


def _mm_residual_kernel(a_ref, w_ref, r_ref, o_ref, wb_ref):
    @pl.when(pl.program_id(1) == 0)
    def _():
        wb_ref[...] = w_ref[...].astype(BF16)

    o_ref[...] = r_ref[...] + jnp.dot(a_ref[...], wb_ref[...], preferred_element_type=F32)


def _mm_residual(a, w_all, layer, res, *, tm=512, tn=512):
    T, K = a.shape
    N = w_all.shape[-1]
    return pl.pallas_call(
        _mm_residual_kernel,
        grid=(N // tn, T // tm),
        in_specs=[pl.BlockSpec((tm, K), lambda j, i: (i, 0)),
                  pl.BlockSpec((None, K, tn), lambda j, i: (layer, 0, j)),
                  pl.BlockSpec((tm, tn), lambda j, i: (i, j))],
        out_specs=pl.BlockSpec((tm, tn), lambda j, i: (i, j)),
        out_shape=jax.ShapeDtypeStruct((T, N), F32),
        scratch_shapes=[pltpu.VMEM((K, tn), BF16)],
        compiler_params=_params(2),
        name="mm_residual",
    )(a, w_all, res)


def _rope_kernel(pos_ref, invf_ref, cos_ref, sin_ref):
    ang = pos_ref[...].astype(F32) * invf_ref[...]
    cos_ref[...] = jnp.cos(ang)
    sin_ref[...] = jnp.sin(ang)


def _rope_tables(pos_col, inv_freq, tm=512):
    T = pos_col.shape[0]
    half = inv_freq.shape[-1]
    out = jax.ShapeDtypeStruct((T, half), F32)
    return pl.pallas_call(
        _rope_kernel,
        grid=(T // tm,),
        in_specs=[pl.BlockSpec((tm, 1), lambda i: (i, 0)),
                  pl.BlockSpec((1, half), lambda i: (0, 0))],
        out_specs=[pl.BlockSpec((tm, half), lambda i: (i, 0))] * 2,
        out_shape=[out, out],
        compiler_params=_params(1),
        name="rope_tables",
    )(pos_col, inv_freq)


def _rotate(t, cos, sin):
    half = t.shape[-1] // 2
    t1, t2 = t[:, :half], t[:, half:]
    return jnp.concatenate([t1 * cos - t2 * sin, t2 * cos + t1 * sin], axis=-1)


def _retention_kernel(proj_ref, cos_ref, sin_ref, din_ref, qd_ref, kd_ref, cd_ref, gn_ref,
                      y_ref, state_ref, *, heads, dk, dv):
    @pl.when(pl.program_id(1) == 0)
    def _():
        state_ref[...] = jnp.zeros_like(state_ref)

    cos = cos_ref[...]
    sin = sin_ref[...]
    k_off, v_off, g_off = heads * dk, 2 * heads * dk, 2 * heads * dk + heads * dv
    for h in range(heads):
        q = proj_ref[:, h * dk:(h + 1) * dk].astype(F32)
        k = proj_ref[:, k_off + h * dk:k_off + (h + 1) * dk].astype(F32)
        v = proj_ref[:, v_off + h * dv:v_off + (h + 1) * dv]
        g = proj_ref[:, g_off + h * dv:g_off + (h + 1) * dv].astype(F32)
        qr = _rotate(q, cos, sin)
        kr = _rotate(k, cos, sin) * (dk ** -0.5)
        qb = qr.astype(BF16)
        scores = lax.dot_general(qb, kr.astype(BF16), (((1,), (1,)), ((), ())),
                                 preferred_element_type=F32) * din_ref[h]
        inner = jnp.dot(scores.astype(BF16), v, preferred_element_type=F32)
        state = state_ref[h]
        cross = jnp.dot(qb, state.astype(BF16), preferred_element_type=F32) * qd_ref[h]
        k_dec = (kr * kd_ref[h]).astype(BF16)
        state_ref[h] = state * cd_ref[h] + lax.dot_general(
            k_dec, v, (((0,), (0,)), ((), ())), preferred_element_type=F32)
        o = inner + cross
        oc = o - jnp.mean(o, axis=-1, keepdims=True)
        var = jnp.mean(oc * oc, axis=-1, keepdims=True)
        on = oc * lax.rsqrt(var + EPS) * gn_ref[:, h * dv:(h + 1) * dv]
        y_ref[:, h * dv:(h + 1) * dv] = (jax.nn.silu(g) * on).astype(y_ref.dtype)


def _retention(proj, cos, sin, gn_all, layer, batch, seq):
    T, width = proj.shape
    H, C = RET_HEADS, CHUNK
    dk = width // (6 * H)
    dv = 2 * dk
    nc = seq // C
    log_gamma = jnp.log1p(-jnp.exp2(-5.0 - jnp.arange(H, dtype=F32)))
    idx = jnp.arange(C, dtype=F32)
    dist = idx[:, None] - idx[None, :]
    decay_in = jnp.where(dist >= 0,
                         jnp.exp(jnp.maximum(dist, 0.0)[None] * log_gamma[:, None, None]), 0.0)
    q_decay = jnp.exp((idx + 1.0)[None, :] * log_gamma[:, None])[..., None]
    k_decay = jnp.exp((C - 1.0 - idx)[None, :] * log_gamma[:, None])[..., None]
    chunk_decay = jnp.exp(C * log_gamma)[:, None, None]

    def whole(shape):
        return pl.BlockSpec(shape, lambda b, c: (0,) * len(shape))

    return pl.pallas_call(
        functools.partial(_retention_kernel, heads=H, dk=dk, dv=dv),
        grid=(batch, nc),
        in_specs=[pl.BlockSpec((C, width), lambda b, c: (b * nc + c, 0)),
                  pl.BlockSpec((C, dk // 2), lambda b, c: (b * nc + c, 0)),
                  pl.BlockSpec((C, dk // 2), lambda b, c: (b * nc + c, 0)),
                  whole((H, C, C)), whole((H, C, 1)), whole((H, C, 1)), whole((H, 1, 1)),
                  pl.BlockSpec((None, 1, H * dv), lambda b, c: (layer, 0, 0))],
        out_specs=pl.BlockSpec((C, H * dv), lambda b, c: (b * nc + c, 0)),
        out_shape=jax.ShapeDtypeStruct((T, H * dv), BF16),
        scratch_shapes=[pltpu.VMEM((H, dk, dv), F32)],
        compiler_params=_params(2),
        name="retention",
    )(proj, cos, sin, decay_in, q_decay, k_decay, chunk_decay, gn_all)


def _sgu_kernel(u_ref, v_ref, lng_ref, lnb_ref, ws_ref, bs_ref, y_ref, *, groups):
    v = v_ref[...].astype(F32)
    vc = v - jnp.mean(v, axis=-1, keepdims=True)
    var = jnp.mean(vc * vc, axis=-1, keepdims=True)
    vn = (vc * lax.rsqrt(var + EPS) * lng_ref[...] + lnb_ref[...]).astype(BF16)
    C = v.shape[0]
    gw = v.shape[1] // groups
    row = lax.broadcasted_iota(jnp.int32, (C, C), 0)
    col = lax.broadcasted_iota(jnp.int32, (C, C), 1)
    for g in range(groups):
        w = jnp.where(col <= row, ws_ref[g], 0.0).astype(BF16)
        mixed = jnp.dot(w, vn[:, g * gw:(g + 1) * gw], preferred_element_type=F32)
        mixed = mixed + bs_ref[:, g:g + 1]
        u = u_ref[:, g * gw:(g + 1) * gw].astype(F32)
        y_ref[:, g * gw:(g + 1) * gw] = (u * mixed).astype(y_ref.dtype)


def _sgu(z, lng_all, lnb_all, ws_all, bs_t, layer):
    T, dffn = z.shape
    half = dffn // 2
    G, C = GMLP_GROUPS, CHUNK
    vec = pl.BlockSpec((None, 1, half), lambda i: (layer, 0, 0))
    return pl.pallas_call(
        functools.partial(_sgu_kernel, groups=G),
        grid=(T // C,),
        in_specs=[pl.BlockSpec((C, half), lambda i: (i, 0)),
                  pl.BlockSpec((C, half), lambda i: (i, 1)),
                  vec, vec,
                  pl.BlockSpec((None, G, C, C), lambda i: (layer, 0, 0, 0)),
                  pl.BlockSpec((None, C, G), lambda i: (layer, 0, 0))],
        out_specs=pl.BlockSpec((C, half), lambda i: (i, 0)),
        out_shape=jax.ShapeDtypeStruct((T, half), BF16),
        compiler_params=_params(1),
        name="sgu",
    )(z, z, lng_all, lnb_all, ws_all, bs_t)


def _fox_qknorm_kernel(qk_ref, qg_ref, kg_ref, o_ref, *, heads, hd):
    scale = hd ** -0.5
    for i in range(2 * heads):
        t = qk_ref[:, i * hd:(i + 1) * hd].astype(F32)
        y = t * lax.rsqrt(jnp.mean(t * t, axis=-1, keepdims=True) + EPS)
        if i < heads:
            y = y * qg_ref[...] * scale
        else:
            y = y * kg_ref[...]
        o_ref[:, i * hd:(i + 1) * hd] = y.astype(o_ref.dtype)


def _fox_qknorm(proj, qg_all, kg_all, layer, tm=512):
    T = proj.shape[0]
    H = FOX_HEADS
    hd = qg_all.shape[-1]
    gspec = pl.BlockSpec((None, 1, hd), lambda i: (layer, 0, 0))
    return pl.pallas_call(
        functools.partial(_fox_qknorm_kernel, heads=H, hd=hd),
        grid=(T // tm,),
        in_specs=[pl.BlockSpec((tm, 2 * H * hd), lambda i: (i, 0)), gspec, gspec],
        out_specs=pl.BlockSpec((tm, 2 * H * hd), lambda i: (i, 0)),
        out_shape=jax.ShapeDtypeStruct((T, 2 * H * hd), BF16),
        compiler_params=_params(1),
        name="fox_qknorm",
    )(proj, qg_all, kg_all)


def _fox_decay_kernel(f_ref, bf_ref, crep_ref, crow_ref, carry_ref, *, heads):
    @pl.when(pl.program_id(1) == 0)
    def _():
        carry_ref[...] = jnp.zeros_like(carry_ref)

    log_f = jax.nn.log_sigmoid(f_ref[...] + bf_ref[...])
    tb = log_f.shape[0]
    row = lax.broadcasted_iota(jnp.int32, (tb, tb), 0)
    col = lax.broadcasted_iota(jnp.int32, (tb, tb), 1)
    tri = jnp.where(col <= row, 1.0, 0.0).astype(BF16)
    hi = log_f.astype(BF16)
    r1 = log_f - hi.astype(F32)
    mid = r1.astype(BF16)
    lo = (r1 - mid.astype(F32)).astype(BF16)
    c = (jnp.dot(tri, hi, preferred_element_type=F32)
         + jnp.dot(tri, mid, preferred_element_type=F32)
         + jnp.dot(tri, lo, preferred_element_type=F32)) + carry_ref[...]
    carry_ref[...] = c[tb - 1:tb, :]
    crow_ref[...] = c.T[:crow_ref.shape[0], :]
    for h in range(heads):
        crep_ref[h] = jnp.broadcast_to(c[:, h:h + 1], (tb, LANES))


def _fox_decay(f_logit, bf_pad, batch, seq, tb=512):
    H = FOX_HEADS
    nb = seq // tb
    return pl.pallas_call(
        functools.partial(_fox_decay_kernel, heads=H),
        grid=(batch, nb),
        in_specs=[pl.BlockSpec((tb, LANES), lambda b, s: (b * nb + s, 0)),
                  pl.BlockSpec((1, LANES), lambda b, s: (0, 0))],
        out_specs=[pl.BlockSpec((None, H, tb, LANES), lambda b, s: (b, 0, s, 0)),
                   pl.BlockSpec((None, H, tb), lambda b, s: (b, 0, s))],
        out_shape=[jax.ShapeDtypeStruct((batch, H, seq, LANES), F32),
                   jax.ShapeDtypeStruct((batch, H, seq), F32)],
        scratch_shapes=[pltpu.VMEM((1, LANES), F32)],
        compiler_params=_params(2),
        name="fox_decay",
    )(f_logit, bf_pad)


def _fox_attn_kernel(qi_tab, ki_tab, q_ref, k_ref, v_ref, ct_ref, cs_ref, g_ref, y_ref,
                     m_ref, l_ref, acc_ref):
    h = pl.program_id(1)
    p = pl.program_id(2)
    qi = qi_tab[p]
    ki = ki_tab[p]

    @pl.when(ki == 0)
    def _():
        m_ref[...] = jnp.full_like(m_ref, -jnp.inf)
        l_ref[...] = jnp.zeros_like(l_ref)
        acc_ref[...] = jnp.zeros_like(acc_ref)

    def step(masked):
        s = lax.dot_general(q_ref[...], k_ref[...], (((1,), (1,)), ((), ())),
                            preferred_element_type=F32)
        s = s + (ct_ref[:, :1] - cs_ref[pl.ds(h, 1), :])
        if masked:
            row = lax.broadcasted_iota(jnp.int32, s.shape, 0)
            col = lax.broadcasted_iota(jnp.int32, s.shape, 1)
            s = jnp.where(col <= row, s, -jnp.inf)
        m_prev = m_ref[...]
        m_new = jnp.maximum(m_prev, jnp.max(s, axis=-1, keepdims=True))
        alpha = jnp.exp(m_prev - m_new)
        pexp = jnp.exp(s - m_new[:, :1])
        l_ref[...] = alpha * l_ref[...] + jnp.sum(pexp, axis=-1, keepdims=True)
        acc_ref[...] = alpha * acc_ref[...] + jnp.dot(
            pexp.astype(BF16), v_ref[...], preferred_element_type=F32)
        m_ref[...] = m_new

    @pl.when(ki < qi)
    def _():
        step(False)

    @pl.when(ki == qi)
    def _():
        step(True)
        o = acc_ref[...] / l_ref[...]
        y_ref[...] = (jax.nn.sigmoid(g_ref[...].astype(F32)) * o).astype(y_ref.dtype)


def _fox_attention(qkn, proj, c_rep, c_row, batch, seq, tq=512):
    T = qkn.shape[0]
    H = FOX_HEADS
    hd = qkn.shape[1] // (2 * H)
    nq = seq // tq
    pairs = [(qi, ki) for qi in range(nq) for ki in range(qi + 1)]
    qi_tab = jnp.asarray([p[0] for p in pairs], jnp.int32)
    ki_tab = jnp.asarray([p[1] for p in pairs], jnp.int32)
    grid_spec = pltpu.PrefetchScalarGridSpec(
        num_scalar_prefetch=2,
        grid=(batch, H, len(pairs)),
        in_specs=[
            pl.BlockSpec((tq, hd), lambda b, h, p, qt, kt: (b * nq + qt[p], h)),
            pl.BlockSpec((tq, hd), lambda b, h, p, qt, kt: (b * nq + kt[p], H + h)),
            pl.BlockSpec((tq, hd), lambda b, h, p, qt, kt: (b * nq + kt[p], 2 * H + h)),
            pl.BlockSpec((None, None, tq, LANES), lambda b, h, p, qt, kt: (b, h, qt[p], 0)),
            pl.BlockSpec((None, H, tq), lambda b, h, p, qt, kt: (b, 0, kt[p])),
            pl.BlockSpec((tq, hd), lambda b, h, p, qt, kt: (b * nq + qt[p], 3 * H + h)),
        ],
        out_specs=pl.BlockSpec((tq, hd), lambda b, h, p, qt, kt: (b * nq + qt[p], h)),
        scratch_shapes=[pltpu.VMEM((tq, LANES), F32), pltpu.VMEM((tq, LANES), F32),
                        pltpu.VMEM((tq, hd), F32)],
    )
    return pl.pallas_call(
        _fox_attn_kernel,
        grid_spec=grid_spec,
        out_shape=jax.ShapeDtypeStruct((T, H * hd), BF16),
        compiler_params=_params(3),
        name="fox_attention",
    )(qi_tab, ki_tab, qkn, qkn, proj, c_rep, c_row, proj)


def kernel(x, positions, mix_norm_g, ffn_norm_g, ret_w_in, ret_gn_g, ret_w_out, gmlp_w_in, gmlp_ln_g, gmlp_ln_b, gmlp_w_s, gmlp_b_s, gmlp_w_out, fox_w_in, fox_b_f, fox_qn_g, fox_kn_g, fox_w_out, ffn_w_gate, ffn_w_up, ffn_w_down):
    B, S, D = x.shape
    T = B * S
    depth = mix_norm_g.shape[0]
    xf = x.reshape(T, D)

    def rows(a):
        return a.reshape(a.shape[0], 1, a.shape[1])

    mix_g, ffn_g = rows(mix_norm_g), rows(ffn_norm_g)
    ret_gn, lng, lnb = rows(ret_gn_g), rows(gmlp_ln_g), rows(gmlp_ln_b)
    qn_g, kn_g = rows(fox_qn_g), rows(fox_kn_g)
    bs_t = jnp.swapaxes(gmlp_b_s, 1, 2)

    half = ret_w_in.shape[-1] // (6 * RET_HEADS) // 2
    inv_freq = (ROPE_BASE ** (-jnp.arange(half, dtype=F32) / half)).reshape(1, half)
    cos, sin = _rope_tables(positions.reshape(T, 1), inv_freq)

    fox_main = 4 * FOX_HEADS * fox_qn_g.shape[-1]
    for i in range(depth):
        kind, j = i % N_MIXERS, i // N_MIXERS
        h = _rmsnorm(xf, mix_g, i)
        if kind == 0:
            proj = _mm(h, ret_w_in, j, ret_w_in.shape[-1], name="mm_ret_in")
            y = _retention(proj, cos, sin, ret_gn, j, B, S)
            xf = _mm_residual(y, ret_w_out, j, xf)
        elif kind == 1:
            z = _mm(h, gmlp_w_in, j, gmlp_w_in.shape[-1], act="gelu", name="mm_gmlp_in")
            y = _sgu(z, lng, lnb, gmlp_w_s, bs_t, j)
            xf = _mm_residual(y, gmlp_w_out, j, xf)
        else:
            proj = _mm(h, fox_w_in, j, fox_main, name="mm_fox_in")
            w_f = jnp.pad(fox_w_in[j][:, fox_main:], ((0, 0), (0, LANES - FOX_HEADS)))[None]
            f_logit = _mm(h, w_f, 0, LANES, out_dtype=F32, tn=LANES, name="mm_fox_forget")
            bf_pad = jnp.pad(fox_b_f[j], (0, LANES - FOX_HEADS)).reshape(1, LANES)
            c_rep, c_row = _fox_decay(f_logit, bf_pad, B, S)
            qkn = _fox_qknorm(proj, qn_g, kn_g, j)
            y = _fox_attention(qkn, proj, c_rep, c_row, B, S)
            xf = _mm_residual(y, fox_w_out, j, xf)
        h = _rmsnorm(xf, ffn_g, i)
        a = _mm_swiglu(h, ffn_w_gate, ffn_w_up, i)
        xf = _mm_residual(a, ffn_w_down, i, xf)
    return xf.reshape(B, S, D)
```

```python
import functools

import jax
import jax.numpy as jnp
from jax import lax
from jax.experimental import pallas as pl
from jax.experimental.pallas import tpu as pltpu

F32 = jnp.float32
BF16 = jnp.bfloat16

EPS = 1e-6
CHUNK = 128
ROPE_BASE = 10000.0
RET_HEADS = 8
GMLP_GROUPS = 8
FOX_HEADS = 16
N_MIXERS = 3
LOG2E = 1.4426950408889634

LANES = 128
VMEM_LIMIT = 56 * 1024 * 1024


def _params(n_axes):
    return pltpu.CompilerParams(
        dimension_semantics=("arbitrary",) * n_axes, vmem_limit_bytes=VMEM_LIMIT)


def _gelu_exact(x):
    return 0.5 * x * (1.0 + lax.erf(x * (2.0 ** -0.5)))


def _prepare_norm_matmul(x_ref, g_ref, w_refs, wb_refs, r_ref):
    j, i = pl.program_id(0), pl.program_id(1)

    @pl.when(i == 0)
    def _():
        for w_ref, wb_ref in zip(w_refs, wb_refs):
            wb_ref[...] = (w_ref[...] * g_ref[...]).astype(BF16)

    @pl.when(j == 0)
    def _():
        xf = x_ref[...].astype(F32)
        r = lax.rsqrt(jnp.mean(xf * xf, axis=-1, keepdims=True) + EPS)
        r_ref[i] = jnp.broadcast_to(r, r_ref.shape[1:])

    return r_ref[i][:, :1]


def _mm_norm_kernel(x_ref, g_ref, w_ref, o_ref, wb_ref, r_ref, *, act):
    r = _prepare_norm_matmul(x_ref, g_ref, (w_ref,), (wb_ref,), r_ref)
    acc = jnp.dot(x_ref[...], wb_ref[...], preferred_element_type=F32) * r
    if act == "gelu":
        acc = _gelu_exact(acc)
    o_ref[...] = acc.astype(o_ref.dtype)


def _mm_norm(xb, g_col, w_all, layer, g_layer, n_out, *, act=None, out_dtype=BF16,
             tm=1024, tn=1024, name="mm_norm"):
    T, K = xb.shape
    return pl.pallas_call(
        functools.partial(_mm_norm_kernel, act=act),
        grid=(n_out // tn, T // tm),
        in_specs=[pl.BlockSpec((tm, K), lambda j, i: (i, 0)),
                  pl.BlockSpec((None, K, 1), lambda j, i: (g_layer, 0, 0)),
                  pl.BlockSpec((None, K, tn), lambda j, i: (layer, 0, j))],
        out_specs=pl.BlockSpec((tm, tn), lambda j, i: (i, j)),
        out_shape=jax.ShapeDtypeStruct((T, n_out), out_dtype),
        scratch_shapes=[pltpu.VMEM((K, tn), BF16), pltpu.VMEM((T // tm, tm, LANES), F32)],
        compiler_params=_params(2),
        name=name,
    )(xb, g_col, w_all)


def _mm_swiglu_kernel(x_ref, g_ref, wg_ref, wu_ref, o_ref, wgb_ref, wub_ref, r_ref):
    r = _prepare_norm_matmul(x_ref, g_ref, (wg_ref, wu_ref), (wgb_ref, wub_ref), r_ref)
    x = x_ref[...]
    gate = jnp.dot(x, wgb_ref[...], preferred_element_type=F32) * r
    up = jnp.dot(x, wub_ref[...], preferred_element_type=F32) * r
    o_ref[...] = (jax.nn.silu(gate) * up).astype(o_ref.dtype)


def _mm_swiglu(xb, g_col, wg_all, wu_all, layer, *, tm=1024, tn=512):
    T, K = xb.shape
    F = wg_all.shape[-1]
    wspec = pl.BlockSpec((None, K, tn), lambda j, i: (layer, 0, j))
    return pl.pallas_call(
        _mm_swiglu_kernel,
        grid=(F // tn, T // tm),
        in_specs=[pl.BlockSpec((tm, K), lambda j, i: (i, 0)),
                  pl.BlockSpec((None, K, 1), lambda j, i: (layer, 0, 0)),
                  wspec, wspec],
        out_specs=pl.BlockSpec((tm, tn), lambda j, i: (i, j)),
        out_shape=jax.ShapeDtypeStruct((T, F), BF16),
        scratch_shapes=[pltpu.VMEM((K, tn), BF16), pltpu.VMEM((K, tn), BF16),
                        pltpu.VMEM((T // tm, tm, LANES), F32)],
        compiler_params=_params(2),
        name="mm_swiglu",
    )(xb, g_col, wg_all, wu_all)


def _mm_residual_kernel(a_ref, w_ref, r_ref, o_ref, ob_ref, wb_ref):
    @pl.when(pl.program_id(1) == 0)
    def _():
        wb_ref[...] = w_ref[...].astype(BF16)

    x_new = r_ref[...] + jnp.dot(a_ref[...], wb_ref[...], preferred_element_type=F32)
    o_ref[...] = x_new
    ob_ref[...] = x_new.astype(BF16)


def _residual_tiles(K):
    if K <= 2048:
        return 1024, 1024
    if K <= 4096:
        return 1024, 512
    return 512, 512


def _mm_residual(a, w_all, layer, res):
    T, K = a.shape
    N = w_all.shape[-1]
    tm, tn = _residual_tiles(K)
    tile = pl.BlockSpec((tm, tn), lambda j, i: (i, j))
    return pl.pallas_call(
        _mm_residual_kernel,
        grid=(N // tn, T // tm),
        in_specs=[pl.BlockSpec((tm, K), lambda j, i: (i, 0)),
                  pl.BlockSpec((None, K, tn), lambda j, i: (layer, 0, j)),
                  tile],
        out_specs=[tile, tile],
        out_shape=[jax.ShapeDtypeStruct((T, N), F32), jax.ShapeDtypeStruct((T, N), BF16)],
        scratch_shapes=[pltpu.VMEM((K, tn), BF16)],
        compiler_params=_params(2),
        name="mm_residual",
    )(a, w_all, res)


def _rope_kernel(pos_ref, invf_ref, cos_ref, sin_ref):
    ang = pos_ref[...].astype(F32) * invf_ref[...]
    cos_ref[...] = jnp.cos(ang)
    sin_ref[...] = jnp.sin(ang)


def _rope_tables(pos_col, inv_freq, tm=512):
    T = pos_col.shape[0]
    half = inv_freq.shape[-1]
    out = jax.ShapeDtypeStruct((T, half), F32)
    return pl.pallas_call(
        _rope_kernel,
        grid=(T // tm,),
        in_specs=[pl.BlockSpec((tm, 1), lambda i: (i, 0)),
                  pl.BlockSpec((1, half), lambda i: (0, 0))],
        out_specs=[pl.BlockSpec((tm, half), lambda i: (i, 0))] * 2,
        out_shape=[out, out],
        compiler_params=_params(1),
        name="rope_tables",
    )(pos_col, inv_freq)


def _rotate(t, cos, sin):
    half = t.shape[-1] // 2
    t1, t2 = t[:, :half], t[:, half:]
    return jnp.concatenate([t1 * cos - t2 * sin, t2 * cos + t1 * sin], axis=-1)


def _retention_kernel(proj_ref, cos_ref, sin_ref, din_ref, qd_ref, kd_ref, cd_ref, gn_ref,
                      y_ref, state_ref, *, heads, dk, dv):
    @pl.when(pl.program_id(1) == 0)
    def _():
        state_ref[...] = jnp.zeros_like(state_ref)

    cos = cos_ref[...]
    sin = sin_ref[...]
    k_off, v_off, g_off = heads * dk, 2 * heads * dk, 2 * heads * dk + heads * dv
    for h in range(heads):
        q = proj_ref[:, h * dk:(h + 1) * dk].astype(F32)
        k = proj_ref[:, k_off + h * dk:k_off + (h + 1) * dk].astype(F32)
        v = proj_ref[:, v_off + h * dv:v_off + (h + 1) * dv]
        g = proj_ref[:, g_off + h * dv:g_off + (h + 1) * dv].astype(F32)
        qr = _rotate(q, cos, sin)
        kr = _rotate(k, cos, sin) * (dk ** -0.5)
        qb = qr.astype(BF16)
        scores = lax.dot_general(qb, kr.astype(BF16), (((1,), (1,)), ((), ())),
                                 preferred_element_type=F32) * din_ref[h]
        inner = jnp.dot(scores.astype(BF16), v, preferred_element_type=F32)
        state = state_ref[h]
        cross = jnp.dot(qb, state.astype(BF16), preferred_element_type=F32) * qd_ref[h]
        k_dec = (kr * kd_ref[h]).astype(BF16)
        state_ref[h] = state * cd_ref[h] + lax.dot_general(
            k_dec, v, (((0,), (0,)), ((), ())), preferred_element_type=F32)
        o = inner + cross
        oc = o - jnp.mean(o, axis=-1, keepdims=True)
        var = jnp.mean(oc * oc, axis=-1, keepdims=True)
        on = oc * lax.rsqrt(var + EPS) * gn_ref[:, h * dv:(h + 1) * dv]
        y_ref[:, h * dv:(h + 1) * dv] = (jax.nn.silu(g) * on).astype(y_ref.dtype)


def _retention(proj, cos, sin, gn_all, layer, batch, seq):
    T, width = proj.shape
    H, C = RET_HEADS, CHUNK
    dk = width // (6 * H)
    dv = 2 * dk
    nc = seq // C
    log_gamma = jnp.log1p(-jnp.exp2(-5.0 - jnp.arange(H, dtype=F32)))
    idx = jnp.arange(C, dtype=F32)
    dist = idx[:, None] - idx[None, :]
    decay_in = jnp.where(dist >= 0,
                         jnp.exp(jnp.maximum(dist, 0.0)[None] * log_gamma[:, None, None]), 0.0)
    q_decay = jnp.exp((idx + 1.0)[None, :] * log_gamma[:, None])[..., None]
    k_decay = jnp.exp((C - 1.0 - idx)[None, :] * log_gamma[:, None])[..., None]
    chunk_decay = jnp.exp(C * log_gamma)[:, None, None]

    def whole(shape):
        return pl.BlockSpec(shape, lambda b, c: (0,) * len(shape))

    return pl.pallas_call(
        functools.partial(_retention_kernel, heads=H, dk=dk, dv=dv),
        grid=(batch, nc),
        in_specs=[pl.BlockSpec((C, width), lambda b, c: (b * nc + c, 0)),
                  pl.BlockSpec((C, dk // 2), lambda b, c: (b * nc + c, 0)),
                  pl.BlockSpec((C, dk // 2), lambda b, c: (b * nc + c, 0)),
                  whole((H, C, C)), whole((H, C, 1)), whole((H, C, 1)), whole((H, 1, 1)),
                  pl.BlockSpec((None, 1, H * dv), lambda b, c: (layer, 0, 0))],
        out_specs=pl.BlockSpec((C, H * dv), lambda b, c: (b * nc + c, 0)),
        out_shape=jax.ShapeDtypeStruct((T, H * dv), BF16),
        scratch_shapes=[pltpu.VMEM((H, dk, dv), F32)],
        compiler_params=_params(2),
        name="retention",
    )(proj, cos, sin, decay_in, q_decay, k_decay, chunk_decay, gn_all)


def _sgu_kernel(u_ref, v_ref, lng_ref, lnb_ref, ws_ref, bs_ref, y_ref, *, groups):
    v = v_ref[...].astype(F32)
    vc = v - jnp.mean(v, axis=-1, keepdims=True)
    var = jnp.mean(vc * vc, axis=-1, keepdims=True)
    vn = (vc * lax.rsqrt(var + EPS) * lng_ref[...] + lnb_ref[...]).astype(BF16)
    C = v.shape[0]
    gw = v.shape[1] // groups
    row = lax.broadcasted_iota(jnp.int32, (C, C), 0)
    col = lax.broadcasted_iota(jnp.int32, (C, C), 1)
    for g in range(groups):
        w = jnp.where(col <= row, ws_ref[g], 0.0).astype(BF16)
        mixed = jnp.dot(w, vn[:, g * gw:(g + 1) * gw], preferred_element_type=F32)
        mixed = mixed + bs_ref[:, g:g + 1]
        u = u_ref[:, g * gw:(g + 1) * gw].astype(F32)
        y_ref[:, g * gw:(g + 1) * gw] = (u * mixed).astype(y_ref.dtype)


def _sgu(z, lng_all, lnb_all, ws_all, bs_t, layer):
    T, dffn = z.shape
    half = dffn // 2
    G, C = GMLP_GROUPS, CHUNK
    vec = pl.BlockSpec((None, 1, half), lambda i: (layer, 0, 0))
    return pl.pallas_call(
        functools.partial(_sgu_kernel, groups=G),
        grid=(T // C,),
        in_specs=[pl.BlockSpec((C, half), lambda i: (i, 0)),
                  pl.BlockSpec((C, half), lambda i: (i, 1)),
                  vec, vec,
                  pl.BlockSpec((None, G, C, C), lambda i: (layer, 0, 0, 0)),
                  pl.BlockSpec((None, C, G), lambda i: (layer, 0, 0))],
        out_specs=pl.BlockSpec((C, half), lambda i: (i, 0)),
        out_shape=jax.ShapeDtypeStruct((T, half), BF16),
        compiler_params=_params(1),
        name="sgu",
    )(z, z, lng_all, lnb_all, ws_all, bs_t)


def _fox_prep_kernel(qk_ref, f_ref, bf_ref, qg_ref, kg_ref, qkn_ref, aqk_ref, carry_ref,
                     *, heads, hd):
    @pl.when(pl.program_id(1) == 0)
    def _():
        carry_ref[...] = jnp.zeros_like(carry_ref)

    q_scale = hd ** -0.5 * LOG2E
    for i in range(2 * heads):
        t = qk_ref[:, i * hd:(i + 1) * hd].astype(F32)
        y = t * lax.rsqrt(jnp.mean(t * t, axis=-1, keepdims=True) + EPS)
        y = y * (qg_ref[...] * q_scale) if i < heads else y * kg_ref[...]
        qkn_ref[:, i * hd:(i + 1) * hd] = y.astype(BF16)

    log_f = jax.nn.log_sigmoid(f_ref[...] + bf_ref[...])
    tb = log_f.shape[0]
    row = lax.broadcasted_iota(jnp.int32, (tb, tb), 0)
    col = lax.broadcasted_iota(jnp.int32, (tb, tb), 1)
    tri = jnp.where(col <= row, 1.0, 0.0).astype(BF16)

    def split3(x):
        hi = x.astype(BF16)
        r1 = x - hi.astype(F32)
        mid = r1.astype(BF16)
        lo = (r1 - mid.astype(F32)).astype(BF16)
        return hi, mid, lo

    c = carry_ref[...] + sum(jnp.dot(tri, part, preferred_element_type=F32)
                             for part in split3(log_f))
    carry_ref[...] = c[tb - 1:tb, :]
    hi, mid, lo = (part.astype(F32) for part in split3(c * LOG2E))
    lane = lax.broadcasted_iota(jnp.int32, (tb, hd), 1)
    for h in range(heads):
        hh, mh, lh = hi[:, h:h + 1], mid[:, h:h + 1], lo[:, h:h + 1]
        aq = jnp.where(lane == 0, hh, jnp.where(lane == 1, mh, jnp.where(
            lane == 2, lh, jnp.where(lane < 6, 1.0, 0.0))))
        ak = jnp.where(lane < 3, 1.0, jnp.where(lane == 3, -hh, jnp.where(
            lane == 4, -mh, jnp.where(lane == 5, -lh, 0.0))))
        aqk_ref[:, h * hd:(h + 1) * hd] = aq.astype(BF16)
        aqk_ref[:, (heads + h) * hd:(heads + h + 1) * hd] = ak.astype(BF16)


def _fox_prep(proj, f_logit, bf_pad, qg_all, kg_all, layer, batch, seq, tb=512):
    T = proj.shape[0]
    H = FOX_HEADS
    hd = qg_all.shape[-1]
    nb = seq // tb
    gspec = pl.BlockSpec((None, 1, hd), lambda b, s: (layer, 0, 0))
    wide = pl.BlockSpec((tb, 2 * H * hd), lambda b, s: (b * nb + s, 0))
    out = jax.ShapeDtypeStruct((T, 2 * H * hd), BF16)
    return pl.pallas_call(
        functools.partial(_fox_prep_kernel, heads=H, hd=hd),
        grid=(batch, nb),
        in_specs=[wide,
                  pl.BlockSpec((tb, LANES), lambda b, s: (b * nb + s, 0)),
                  pl.BlockSpec((1, LANES), lambda b, s: (0, 0)),
                  gspec, gspec],
        out_specs=[wide, wide],
        out_shape=[out, out],
        scratch_shapes=[pltpu.VMEM((1, LANES), F32)],
        compiler_params=_params(2),
        name="fox_prep",
    )(proj, f_logit, bf_pad, qg_all, kg_all)


def _fox_attn_kernel(qi_tab, ki_tab, q_ref, aq_ref, k_ref, ak_ref, v_ref, g_ref, y_ref,
                     m_ref, acc_ref, *, hg, hd):
    p = pl.program_id(2)
    qi = qi_tab[p]
    ki = ki_tab[p]

    @pl.when(ki == 0)
    def _():
        m_ref[...] = jnp.full_like(m_ref, -jnp.inf)
        acc_ref[...] = jnp.zeros_like(acc_ref)

    def step(masked):
        tk = k_ref.shape[0]
        ones_col = jnp.where(lax.broadcasted_iota(jnp.int32, (tk, hd), 1) == 0,
                             1.0, 0.0).astype(BF16)
        for h in range(hg):
            sl = slice(h * hd, (h + 1) * hd)
            qc = jnp.concatenate([q_ref[:, sl], aq_ref[:, sl]], axis=1)
            kc = jnp.concatenate([k_ref[:, sl], ak_ref[:, sl]], axis=1)
            s = lax.dot_general(qc, kc, (((1,), (1,)), ((), ())),
                                preferred_element_type=F32)
            if masked:
                row = lax.broadcasted_iota(jnp.int32, s.shape, 0)
                col = lax.broadcasted_iota(jnp.int32, s.shape, 1)
                s = jnp.where(col <= row, s, -jnp.inf)
            m_prev = m_ref[h]
            m_new = jnp.maximum(m_prev, jnp.max(s, axis=-1, keepdims=True))
            alpha = jnp.exp2(m_prev - m_new)
            pexp = jnp.exp2(s - m_new[:, :1]).astype(BF16)
            vc = jnp.concatenate([v_ref[:, sl], ones_col], axis=1)
            pv = jnp.dot(pexp, vc, preferred_element_type=F32)
            acc_ref[h] = jnp.concatenate([alpha, alpha], axis=1) * acc_ref[h] + pv
            m_ref[h] = m_new

    @pl.when(ki < qi)
    def _():
        step(False)

    @pl.when(ki == qi)
    def _():
        step(True)
        for h in range(hg):
            sl = slice(h * hd, (h + 1) * hd)
            acc = acc_ref[h]
            o = acc[:, :hd] / acc[:, hd:hd + 1]
            y_ref[:, sl] = (jax.nn.sigmoid(g_ref[:, sl].astype(F32)) * o).astype(y_ref.dtype)


def _fox_attention(qkn, aqk, proj, batch, seq, tq=512, hg=8):
    T = qkn.shape[0]
    H = FOX_HEADS
    hd = qkn.shape[1] // (2 * H)
    nq = seq // tq
    ng = H // hg
    pairs = [(qi, ki) for qi in range(nq) for ki in range(qi + 1)]
    qi_tab = jnp.asarray([p[0] for p in pairs], jnp.int32)
    ki_tab = jnp.asarray([p[1] for p in pairs], jnp.int32)

    def q_side(col0):
        return pl.BlockSpec((tq, hg * hd), lambda b, g, p, qt, kt: (b * nq + qt[p], col0 + g))

    def k_side(col0):
        return pl.BlockSpec((tq, hg * hd), lambda b, g, p, qt, kt: (b * nq + kt[p], col0 + g))

    grid_spec = pltpu.PrefetchScalarGridSpec(
        num_scalar_prefetch=2,
        grid=(batch, ng, len(pairs)),
        in_specs=[q_side(0), q_side(0), k_side(ng), k_side(ng), k_side(2 * ng), q_side(3 * ng)],
        out_specs=q_side(0),
        scratch_shapes=[pltpu.VMEM((hg, tq, LANES), F32), pltpu.VMEM((hg, tq, 2 * hd), F32)],
    )
    return pl.pallas_call(
        functools.partial(_fox_attn_kernel, hg=hg, hd=hd),
        grid_spec=grid_spec,
        out_shape=jax.ShapeDtypeStruct((T, H * hd), BF16),
        compiler_params=_params(3),
        name="fox_attention",
    )(qi_tab, ki_tab, qkn, aqk, qkn, aqk, proj, proj)


def kernel(x, positions, mix_norm_g, ffn_norm_g, ret_w_in, ret_gn_g, ret_w_out, gmlp_w_in, gmlp_ln_g, gmlp_ln_b, gmlp_w_s, gmlp_b_s, gmlp_w_out, fox_w_in, fox_b_f, fox_qn_g, fox_kn_g, fox_w_out, ffn_w_gate, ffn_w_up, ffn_w_down):
    B, S, D = x.shape
    T = B * S
    depth = mix_norm_g.shape[0]
    xf = x.reshape(T, D)
    xb = xf.astype(BF16)

    def rows(a):
        return a.reshape(a.shape[0], 1, a.shape[1])

    mix_g, ffn_g = mix_norm_g[..., None], ffn_norm_g[..., None]
    ret_gn, lng, lnb = rows(ret_gn_g), rows(gmlp_ln_g), rows(gmlp_ln_b)
    qn_g, kn_g = rows(fox_qn_g), rows(fox_kn_g)
    bs_t = jnp.swapaxes(gmlp_b_s, 1, 2)

    half = ret_w_in.shape[-1] // (6 * RET_HEADS) // 2
    inv_freq = (ROPE_BASE ** (-jnp.arange(half, dtype=F32) / half)).reshape(1, half)
    cos, sin = _rope_tables(positions.reshape(T, 1), inv_freq)

    fox_main = 4 * FOX_HEADS * fox_qn_g.shape[-1]
    for i in range(depth):
        kind, j = i % N_MIXERS, i // N_MIXERS
        if kind == 0:
            proj = _mm_norm(xb, mix_g, ret_w_in, j, i, ret_w_in.shape[-1], name="mm_ret_in")
            y = _retention(proj, cos, sin, ret_gn, j, B, S)
            xf, xb = _mm_residual(y, ret_w_out, j, xf)
        elif kind == 1:
            z = _mm_norm(xb, mix_g, gmlp_w_in, j, i, gmlp_w_in.shape[-1], act="gelu",
                         name="mm_gmlp_in")
            y = _sgu(z, lng, lnb, gmlp_w_s, bs_t, j)
            xf, xb = _mm_residual(y, gmlp_w_out, j, xf)
        else:
            proj = _mm_norm(xb, mix_g, fox_w_in, j, i, fox_main, name="mm_fox_in")
            w_f = jnp.pad(fox_w_in[j][:, fox_main:], ((0, 0), (0, LANES - FOX_HEADS)))[None]
            f_logit = _mm_norm(xb, mix_g, w_f, 0, i, LANES, out_dtype=F32, tn=LANES,
                               name="mm_fox_forget")
            bf_pad = jnp.pad(fox_b_f[j], (0, LANES - FOX_HEADS)).reshape(1, LANES)
            qkn, aqk = _fox_prep(proj, f_logit, bf_pad, qn_g, kn_g, j, B, S)
            y = _fox_attention(qkn, aqk, proj, B, S)
            xf, xb = _mm_residual(y, fox_w_out, j, xf)
        a = _mm_swiglu(xb, ffn_g, ffn_w_gate, ffn_w_up, i)
        xf, xb = _mm_residual(a, ffn_w_down, i, xf)
    return xf.reshape(B, S, D)
```

```python
import functools

import jax
import jax.numpy as jnp
from jax import lax
from jax.experimental import pallas as pl
from jax.experimental.pallas import tpu as pltpu

F32 = jnp.float32
BF16 = jnp.bfloat16

EPS = 1e-6
CHUNK = 128
RET_BLOCK = 256
ROPE_BASE = 10000.0
RET_HEADS = 8
GMLP_GROUPS = 8
FOX_HEADS = 16
N_MIXERS = 3
LOG2E = 1.4426950408889634

LANES = 128
VMEM_LIMIT = 56 * 1024 * 1024


def _params(n_axes):
    return pltpu.CompilerParams(
        dimension_semantics=("arbitrary",) * n_axes, vmem_limit_bytes=VMEM_LIMIT)


def _gelu_exact(x):
    return 0.5 * x * (1.0 + lax.erf(x * (2.0 ** -0.5)))


def _prepare_norm_matmul(x_ref, g_ref, w_refs, wb_refs, r_ref):
    j, i = pl.program_id(0), pl.program_id(1)

    @pl.when(i == 0)
    def _():
        for w_ref, wb_ref in zip(w_refs, wb_refs):
            wb_ref[...] = (w_ref[...] * g_ref[...]).astype(BF16)

    @pl.when(j == 0)
    def _():
        xf = x_ref[...].astype(F32)
        r = lax.rsqrt(jnp.mean(xf * xf, axis=-1, keepdims=True) + EPS)
        r_ref[i] = jnp.broadcast_to(r, r_ref.shape[1:])

    return r_ref[i][:, :1]


def _mm_norm_kernel(*refs, epilogue, transposed, cfg):
    x_ref, g_ref, w_ref = refs[:3]
    o_ref, wb_ref, r_ref = refs[-3:]
    extra = refs[3:-3]
    r = _prepare_norm_matmul(x_ref, g_ref, (w_ref,), (wb_ref,), r_ref)
    if transposed:
        acc = lax.dot_general(x_ref[...], wb_ref[...], (((1,), (1,)), ((), ())),
                              preferred_element_type=F32) * r
    else:
        acc = jnp.dot(x_ref[...], wb_ref[...], preferred_element_type=F32) * r
    j = pl.program_id(0)
    tn = o_ref.shape[1]
    if epilogue == "gelu":
        o_ref[...] = _gelu_exact(acc).astype(o_ref.dtype)
    elif epilogue == "rope":
        cos_ref, sin_ref = extra
        dk, k_tile0 = cfg
        half = dk // 2
        scale = jnp.where(j >= k_tile0, dk ** -0.5, 1.0)
        cos = cos_ref[...] * scale
        sin = sin_ref[...] * scale
        for h in range(tn // dk):
            t1 = acc[:, h * dk:h * dk + half]
            t2 = acc[:, h * dk + half:(h + 1) * dk]
            o_ref[:, h * dk:h * dk + half] = (t1 * cos - t2 * sin).astype(o_ref.dtype)
            o_ref[:, h * dk + half:(h + 1) * dk] = (t2 * cos + t1 * sin).astype(o_ref.dtype)
    elif epilogue == "qknorm":
        qg_ref, kg_ref = extra
        hd, k_tile0, q_scale = cfg
        gain = jnp.where(j >= k_tile0, kg_ref[...], qg_ref[...] * q_scale)
        for h in range(tn // hd):
            t = acc[:, h * hd:(h + 1) * hd]
            y = t * lax.rsqrt(jnp.mean(t * t, axis=-1, keepdims=True) + EPS) * gain
            o_ref[:, h * hd:(h + 1) * hd] = y.astype(o_ref.dtype)
    else:
        o_ref[...] = acc.astype(o_ref.dtype)


def _mm_norm(xb, gain, w_all, layer, g_layer, n_out, *, col0=0, transposed=False, epilogue=None,
             extra=(), extra_specs=(), cfg=None, out_dtype=BF16, tm=1024, tn=1024, name="mm_norm"):
    T, K = xb.shape
    if transposed:
        g_spec = pl.BlockSpec((None, 1, K), lambda j, i: (g_layer, 0, 0))
        w_spec = pl.BlockSpec((None, tn, K), lambda j, i: (layer, col0 + j, 0))
        wb_shape = (tn, K)
    else:
        g_spec = pl.BlockSpec((None, K, 1), lambda j, i: (g_layer, 0, 0))
        w_spec = pl.BlockSpec((None, K, tn), lambda j, i: (layer, 0, col0 + j))
        wb_shape = (K, tn)
    return pl.pallas_call(
        functools.partial(_mm_norm_kernel, epilogue=epilogue, transposed=transposed, cfg=cfg),
        grid=(n_out // tn, T // tm),
        in_specs=[pl.BlockSpec((tm, K), lambda j, i: (i, 0)), g_spec, w_spec, *extra_specs],
        out_specs=pl.BlockSpec((tm, tn), lambda j, i: (i, j)),
        out_shape=jax.ShapeDtypeStruct((T, n_out), out_dtype),
        scratch_shapes=[pltpu.VMEM(wb_shape, BF16), pltpu.VMEM((T // tm, tm, LANES), F32)],
        compiler_params=_params(2),
        name=name,
    )(xb, gain, w_all, *extra)


def _mm_swiglu_kernel(x_ref, g_ref, wg_ref, wu_ref, o_ref, wgb_ref, wub_ref, r_ref):
    r = _prepare_norm_matmul(x_ref, g_ref, (wg_ref, wu_ref), (wgb_ref, wub_ref), r_ref)
    x = x_ref[...]
    gate = jnp.dot(x, wgb_ref[...], preferred_element_type=F32) * r
    up = jnp.dot(x, wub_ref[...], preferred_element_type=F32) * r
    o_ref[...] = (jax.nn.silu(gate) * up).astype(o_ref.dtype)


def _mm_swiglu(xb, g_col, wg_all, wu_all, layer, *, tm=1024, tn=512):
    T, K = xb.shape
    F = wg_all.shape[-1]
    wspec = pl.BlockSpec((None, K, tn), lambda j, i: (layer, 0, j))
    return pl.pallas_call(
        _mm_swiglu_kernel,
        grid=(F // tn, T // tm),
        in_specs=[pl.BlockSpec((tm, K), lambda j, i: (i, 0)),
                  pl.BlockSpec((None, K, 1), lambda j, i: (layer, 0, 0)),
                  wspec, wspec],
        out_specs=pl.BlockSpec((tm, tn), lambda j, i: (i, j)),
        out_shape=jax.ShapeDtypeStruct((T, F), BF16),
        scratch_shapes=[pltpu.VMEM((K, tn), BF16), pltpu.VMEM((K, tn), BF16),
                        pltpu.VMEM((T // tm, tm, LANES), F32)],
        compiler_params=_params(2),
        name="mm_swiglu",
    )(xb, g_col, wg_all, wu_all)


def _mm_residual_kernel(a_ref, w_ref, r_ref, o_ref, ob_ref, wb_ref):
    @pl.when(pl.program_id(1) == 0)
    def _():
        wb_ref[...] = w_ref[...].astype(BF16)

    x_new = r_ref[...] + jnp.dot(a_ref[...], wb_ref[...], preferred_element_type=F32)
    o_ref[...] = x_new
    ob_ref[...] = x_new.astype(BF16)


def _residual_tiles(K):
    if K <= 2048:
        return 1024, 1024
    if K <= 4096:
        return 1024, 512
    return 512, 512


def _mm_residual(a, w_all, layer, res):
    T, K = a.shape
    N = w_all.shape[-1]
    tm, tn = _residual_tiles(K)
    tile = pl.BlockSpec((tm, tn), lambda j, i: (i, j))
    return pl.pallas_call(
        _mm_residual_kernel,
        grid=(N // tn, T // tm),
        in_specs=[pl.BlockSpec((tm, K), lambda j, i: (i, 0)),
                  pl.BlockSpec((None, K, tn), lambda j, i: (layer, 0, j)),
                  tile],
        out_specs=[tile, tile],
        out_shape=[jax.ShapeDtypeStruct((T, N), F32), jax.ShapeDtypeStruct((T, N), BF16)],
        scratch_shapes=[pltpu.VMEM((K, tn), BF16)],
        compiler_params=_params(2),
        name="mm_residual",
    )(a, w_all, res)


def _rope_kernel(pos_ref, invf_ref, cos_ref, sin_ref):
    ang = pos_ref[...].astype(F32) * invf_ref[...]
    cos_ref[...] = jnp.cos(ang)
    sin_ref[...] = jnp.sin(ang)


def _rope_tables(pos_col, inv_freq, tm=512):
    T = pos_col.shape[0]
    half = inv_freq.shape[-1]
    out = jax.ShapeDtypeStruct((T, half), F32)
    return pl.pallas_call(
        _rope_kernel,
        grid=(T // tm,),
        in_specs=[pl.BlockSpec((tm, 1), lambda i: (i, 0)),
                  pl.BlockSpec((1, half), lambda i: (0, 0))],
        out_specs=[pl.BlockSpec((tm, half), lambda i: (i, 0))] * 2,
        out_shape=[out, out],
        compiler_params=_params(1),
        name="rope_tables",
    )(pos_col, inv_freq)


def _retention_kernel(qk_ref, vg_ref, din_ref, qd_ref, kd_ref, cd_ref, gn_ref,
                      y_ref, state_ref, *, heads, dk, dv):
    @pl.when(pl.program_id(1) == 0)
    def _():
        state_ref[...] = jnp.zeros_like(state_ref)

    k_off, g_off = heads * dk, heads * dv
    for h in range(heads):
        qb = qk_ref[:, h * dk:(h + 1) * dk]
        kb = qk_ref[:, k_off + h * dk:k_off + (h + 1) * dk]
        v = vg_ref[:, h * dv:(h + 1) * dv]
        g = vg_ref[:, g_off + h * dv:g_off + (h + 1) * dv].astype(F32)
        scores = lax.dot_general(qb, kb, (((1,), (1,)), ((), ())),
                                 preferred_element_type=F32) * din_ref[h]
        inner = jnp.dot(scores.astype(BF16), v, preferred_element_type=F32)
        state = state_ref[h]
        cross = jnp.dot(qb, state.astype(BF16), preferred_element_type=F32) * qd_ref[h]
        k_dec = (kb.astype(F32) * kd_ref[h]).astype(BF16)
        state_ref[h] = state * cd_ref[h] + lax.dot_general(
            k_dec, v, (((0,), (0,)), ((), ())), preferred_element_type=F32)
        o = inner + cross
        oc = o - jnp.mean(o, axis=-1, keepdims=True)
        var = jnp.mean(oc * oc, axis=-1, keepdims=True)
        on = oc * lax.rsqrt(var + EPS) * gn_ref[:, h * dv:(h + 1) * dv]
        y_ref[:, h * dv:(h + 1) * dv] = (jax.nn.silu(g) * on).astype(y_ref.dtype)


def _retention(qk, vg, gn_all, layer, batch, seq):
    T = qk.shape[0]
    H, C = RET_HEADS, RET_BLOCK
    dk = qk.shape[1] // (2 * H)
    dv = vg.shape[1] // (2 * H)
    nc = seq // C
    log_gamma = jnp.log1p(-jnp.exp2(-5.0 - jnp.arange(H, dtype=F32)))
    idx = jnp.arange(C, dtype=F32)
    dist = idx[:, None] - idx[None, :]
    decay_in = jnp.where(dist >= 0,
                         jnp.exp(jnp.maximum(dist, 0.0)[None] * log_gamma[:, None, None]), 0.0)
    q_decay = jnp.exp((idx + 1.0)[None, :] * log_gamma[:, None])[..., None]
    k_decay = jnp.exp((C - 1.0 - idx)[None, :] * log_gamma[:, None])[..., None]
    chunk_decay = jnp.exp(C * log_gamma)[:, None, None]

    def whole(shape):
        return pl.BlockSpec(shape, lambda b, c: (0,) * len(shape))

    return pl.pallas_call(
        functools.partial(_retention_kernel, heads=H, dk=dk, dv=dv),
        grid=(batch, nc),
        in_specs=[pl.BlockSpec((C, 2 * H * dk), lambda b, c: (b * nc + c, 0)),
                  pl.BlockSpec((C, 2 * H * dv), lambda b, c: (b * nc + c, 0)),
                  whole((H, C, C)), whole((H, C, 1)), whole((H, C, 1)), whole((H, 1, 1)),
                  pl.BlockSpec((None, 1, H * dv), lambda b, c: (layer, 0, 0))],
        out_specs=pl.BlockSpec((C, H * dv), lambda b, c: (b * nc + c, 0)),
        out_shape=jax.ShapeDtypeStruct((T, H * dv), BF16),
        scratch_shapes=[pltpu.VMEM((H, dk, dv), F32)],
        compiler_params=_params(2),
        name="retention",
    )(qk, vg, decay_in, q_decay, k_decay, chunk_decay, gn_all)


def _sgu_kernel(u_ref, v_ref, lng_ref, lnb_ref, ws_ref, bs_ref, y_ref, *, groups):
    v = v_ref[...].astype(F32)
    vc = v - jnp.mean(v, axis=-1, keepdims=True)
    var = jnp.mean(vc * vc, axis=-1, keepdims=True)
    vn = (vc * lax.rsqrt(var + EPS) * lng_ref[...] + lnb_ref[...]).astype(BF16)
    C = v.shape[0]
    gw = v.shape[1] // groups
    row = lax.broadcasted_iota(jnp.int32, (C, C), 0)
    col = lax.broadcasted_iota(jnp.int32, (C, C), 1)
    for g in range(groups):
        w = jnp.where(col <= row, ws_ref[g], 0.0).astype(BF16)
        mixed = jnp.dot(w, vn[:, g * gw:(g + 1) * gw], preferred_element_type=F32)
        mixed = mixed + bs_ref[:, g:g + 1]
        u = u_ref[:, g * gw:(g + 1) * gw].astype(F32)
        y_ref[:, g * gw:(g + 1) * gw] = (u * mixed).astype(y_ref.dtype)


def _sgu(z, lng_all, lnb_all, ws_all, bs_t, layer):
    T, dffn = z.shape
    half = dffn // 2
    G, C = GMLP_GROUPS, CHUNK
    vec = pl.BlockSpec((None, 1, half), lambda i: (layer, 0, 0))
    return pl.pallas_call(
        functools.partial(_sgu_kernel, groups=G),
        grid=(T // C,),
        in_specs=[pl.BlockSpec((C, half), lambda i: (i, 0)),
                  pl.BlockSpec((C, half), lambda i: (i, 1)),
                  vec, vec,
                  pl.BlockSpec((None, G, C, C), lambda i: (layer, 0, 0, 0)),
                  pl.BlockSpec((None, C, G), lambda i: (layer, 0, 0))],
        out_specs=pl.BlockSpec((C, half), lambda i: (i, 0)),
        out_shape=jax.ShapeDtypeStruct((T, half), BF16),
        compiler_params=_params(1),
        name="sgu",
    )(z, z, lng_all, lnb_all, ws_all, bs_t)


def _fox_prep_kernel(f_ref, bf_ref, aqk_ref, carry_ref, *, hd):
    @pl.when(pl.program_id(1) == 0)
    def _():
        carry_ref[...] = jnp.zeros_like(carry_ref)

    log_f = jax.nn.log_sigmoid(f_ref[...] + bf_ref[...])
    tb, heads = log_f.shape
    row = lax.broadcasted_iota(jnp.int32, (tb, tb), 0)
    col = lax.broadcasted_iota(jnp.int32, (tb, tb), 1)
    tri = jnp.where(col <= row, 1.0, 0.0).astype(BF16)

    def split3(x):
        hi = x.astype(BF16)
        r1 = x - hi.astype(F32)
        mid = r1.astype(BF16)
        lo = (r1 - mid.astype(F32)).astype(BF16)
        return hi, mid, lo

    c = carry_ref[...] + sum(jnp.dot(tri, part, preferred_element_type=F32)
                             for part in split3(log_f))
    carry_ref[...] = c[tb - 1:tb, :]
    hi, mid, lo = (part.astype(F32) for part in split3(c * LOG2E))
    lane = lax.broadcasted_iota(jnp.int32, (tb, hd), 1)
    for h in range(heads):
        hh, mh, lh = hi[:, h:h + 1], mid[:, h:h + 1], lo[:, h:h + 1]
        aq = jnp.where(lane == 0, hh, jnp.where(lane == 1, mh, jnp.where(
            lane == 2, lh, jnp.where(lane < 6, 1.0, 0.0))))
        ak = jnp.where(lane < 3, 1.0, jnp.where(lane == 3, -hh, jnp.where(
            lane == 4, -mh, jnp.where(lane == 5, -lh, 0.0))))
        aqk_ref[:, h * hd:(h + 1) * hd] = aq.astype(BF16)
        aqk_ref[:, (heads + h) * hd:(heads + h + 1) * hd] = ak.astype(BF16)


def _fox_prep(f_logit, bf_all, layer, hd, batch, seq, tb=512):
    T, H = f_logit.shape
    nb = seq // tb
    return pl.pallas_call(
        functools.partial(_fox_prep_kernel, hd=hd),
        grid=(batch, nb),
        in_specs=[pl.BlockSpec((tb, H), lambda b, s: (b * nb + s, 0)),
                  pl.BlockSpec((None, 1, H), lambda b, s: (layer, 0, 0))],
        out_specs=pl.BlockSpec((tb, 2 * H * hd), lambda b, s: (b * nb + s, 0)),
        out_shape=jax.ShapeDtypeStruct((T, 2 * H * hd), BF16),
        scratch_shapes=[pltpu.VMEM((1, H), F32)],
        compiler_params=_params(2),
        name="fox_prep",
    )(f_logit, bf_all)


def _fox_attn_kernel(qi_tab, ki_tab, q_ref, aq_ref, k_ref, ak_ref, v_ref, g_ref, y_ref,
                     m_ref, acc_ref, *, hg, hd):
    p = pl.program_id(2)
    qi = qi_tab[p]
    ki = ki_tab[p]

    @pl.when(ki == 0)
    def _():
        m_ref[...] = jnp.full_like(m_ref, -jnp.inf)
        acc_ref[...] = jnp.zeros_like(acc_ref)

    def step(masked):
        tk = k_ref.shape[0]
        ones_col = jnp.where(lax.broadcasted_iota(jnp.int32, (tk, hd), 1) == 0,
                             1.0, 0.0).astype(BF16)
        for h in range(hg):
            sl = slice(h * hd, (h + 1) * hd)
            qc = jnp.concatenate([q_ref[:, sl], aq_ref[:, sl]], axis=1)
            kc = jnp.concatenate([k_ref[:, sl], ak_ref[:, sl]], axis=1)
            s = lax.dot_general(qc, kc, (((1,), (1,)), ((), ())),
                                preferred_element_type=F32)
            if masked:
                row = lax.broadcasted_iota(jnp.int32, s.shape, 0)
                col = lax.broadcasted_iota(jnp.int32, s.shape, 1)
                s = jnp.where(col <= row, s, -jnp.inf)
            m_prev = m_ref[h]
            m_new = jnp.maximum(m_prev, jnp.max(s, axis=-1, keepdims=True))
            alpha = jnp.exp2(m_prev - m_new)
            pexp = jnp.exp2(s - m_new[:, :1]).astype(BF16)
            vc = jnp.concatenate([v_ref[:, sl], ones_col], axis=1)
            pv = jnp.dot(pexp, vc, preferred_element_type=F32)
            acc_ref[h] = jnp.concatenate([alpha, alpha], axis=1) * acc_ref[h] + pv
            m_ref[h] = m_new

    @pl.when(ki < qi)
    def _():
        step(False)

    @pl.when(ki == qi)
    def _():
        step(True)
        for h in range(hg):
            sl = slice(h * hd, (h + 1) * hd)
            acc = acc_ref[h]
            o = acc[:, :hd] / acc[:, hd:hd + 1]
            y_ref[:, sl] = (jax.nn.sigmoid(g_ref[:, sl].astype(F32)) * o).astype(y_ref.dtype)


def _fox_attention(qkn, aqk, vg, batch, seq, tq=512, hg=8):
    T = qkn.shape[0]
    H = FOX_HEADS
    hd = qkn.shape[1] // (2 * H)
    nq = seq // tq
    ng = H // hg
    pairs = [(qi, ki) for qi in range(nq) for ki in range(qi + 1)]
    qi_tab = jnp.asarray([p[0] for p in pairs], jnp.int32)
    ki_tab = jnp.asarray([p[1] for p in pairs], jnp.int32)

    def q_side(col0):
        return pl.BlockSpec((tq, hg * hd), lambda b, g, p, qt, kt: (b * nq + qt[p], col0 + g))

    def k_side(col0):
        return pl.BlockSpec((tq, hg * hd), lambda b, g, p, qt, kt: (b * nq + kt[p], col0 + g))

    grid_spec = pltpu.PrefetchScalarGridSpec(
        num_scalar_prefetch=2,
        grid=(batch, ng, len(pairs)),
        in_specs=[q_side(0), q_side(0), k_side(ng), k_side(ng), k_side(0), q_side(ng)],
        out_specs=q_side(0),
        scratch_shapes=[pltpu.VMEM((hg, tq, LANES), F32), pltpu.VMEM((hg, tq, 2 * hd), F32)],
    )
    return pl.pallas_call(
        functools.partial(_fox_attn_kernel, hg=hg, hd=hd),
        grid_spec=grid_spec,
        out_shape=jax.ShapeDtypeStruct((T, H * hd), BF16),
        compiler_params=_params(3),
        name="fox_attention",
    )(qi_tab, ki_tab, qkn, aqk, qkn, aqk, vg, vg)


def kernel(x, positions, mix_norm_g, ffn_norm_g, ret_w_in, ret_gn_g, ret_w_out, gmlp_w_in, gmlp_ln_g, gmlp_ln_b, gmlp_w_s, gmlp_b_s, gmlp_w_out, fox_w_in, fox_b_f, fox_qn_g, fox_kn_g, fox_w_out, ffn_w_gate, ffn_w_up, ffn_w_down):
    B, S, D = x.shape
    T = B * S
    depth = mix_norm_g.shape[0]
    xf = x.reshape(T, D)
    xb = xf.astype(BF16)

    def rows(a):
        return a.reshape(a.shape[0], 1, a.shape[1])

    mix_g, ffn_g = mix_norm_g[..., None], ffn_norm_g[..., None]
    mix_g_row = rows(mix_norm_g)
    ret_gn, lng, lnb = rows(ret_gn_g), rows(gmlp_ln_g), rows(gmlp_ln_b)
    qn_g, kn_g, fox_bf = rows(fox_qn_g), rows(fox_kn_g), rows(fox_b_f)
    bs_t = jnp.swapaxes(gmlp_b_s, 1, 2)
    fox_w_t = jnp.swapaxes(fox_w_in, 1, 2)

    tm = tn = 1024
    ret_dk = ret_w_in.shape[-1] // (6 * RET_HEADS)
    ret_qk = 2 * RET_HEADS * ret_dk
    inv_freq = (ROPE_BASE ** (-jnp.arange(ret_dk // 2, dtype=F32) / (ret_dk // 2))).reshape(1, -1)
    cos, sin = _rope_tables(positions.reshape(T, 1), inv_freq)
    rope_spec = pl.BlockSpec((tm, ret_dk // 2), lambda j, i: (i, 0))

    hd = fox_qn_g.shape[-1]
    fox_qk = 2 * FOX_HEADS * hd
    for i in range(depth):
        kind, j = i % N_MIXERS, i // N_MIXERS
        if kind == 0:
            qk = _mm_norm(xb, mix_g, ret_w_in, j, i, ret_qk, epilogue="rope", extra=(cos, sin),
                          extra_specs=(rope_spec, rope_spec), cfg=(ret_dk, ret_qk // 2 // tn),
                          tm=tm, tn=tn, name="mm_ret_qk")
            vg = _mm_norm(xb, mix_g, ret_w_in, j, i, ret_w_in.shape[-1] - ret_qk,
                          col0=ret_qk // tn, tm=tm, tn=tn, name="mm_ret_vg")
            y = _retention(qk, vg, ret_gn, j, B, S)
            xf, xb = _mm_residual(y, ret_w_out, j, xf)
        elif kind == 1:
            z = _mm_norm(xb, mix_g, gmlp_w_in, j, i, gmlp_w_in.shape[-1], epilogue="gelu",
                         name="mm_gmlp_in")
            y = _sgu(z, lng, lnb, gmlp_w_s, bs_t, j)
            xf, xb = _mm_residual(y, gmlp_w_out, j, xf)
        else:
            gain_spec = pl.BlockSpec((None, 1, hd), lambda jj, ii: (j, 0, 0))
            qkn = _mm_norm(xb, mix_g_row, fox_w_t, j, i, fox_qk, transposed=True,
                           epilogue="qknorm", extra=(qn_g, kn_g), extra_specs=(gain_spec, gain_spec),
                           cfg=(hd, fox_qk // 2 // tn, hd ** -0.5 * LOG2E),
                           tm=tm, tn=tn, name="mm_fox_qk")
            vg = _mm_norm(xb, mix_g_row, fox_w_t, j, i, fox_qk, col0=fox_qk // tn,
                          transposed=True, tm=tm, tn=tn, name="mm_fox_vg")
            f_logit = _mm_norm(xb, mix_g_row, fox_w_t, j, i, FOX_HEADS,
                               col0=2 * fox_qk // FOX_HEADS, transposed=True, out_dtype=F32,
                               tm=tm, tn=FOX_HEADS, name="mm_fox_forget")
            aqk = _fox_prep(f_logit, fox_bf, j, hd, B, S)
            y = _fox_attention(qkn, aqk, vg, B, S)
            xf, xb = _mm_residual(y, fox_w_out, j, xf)
        a = _mm_swiglu(xb, ffn_g, ffn_w_gate, ffn_w_up, i)
        xf, xb = _mm_residual(a, ffn_w_down, i, xf)
    return xf.reshape(B, S, D)
```

```python
import functools

import jax
import jax.numpy as jnp
from jax import lax
from jax.experimental import pallas as pl
from jax.experimental.pallas import tpu as pltpu

F32 = jnp.float32
BF16 = jnp.bfloat16

EPS = 1e-6
CHUNK = 128
RET_BLOCK = 256
ROPE_BASE = 10000.0
RET_HEADS = 8
GMLP_GROUPS = 8
FOX_HEADS = 16
N_MIXERS = 3
LOG2E = 1.4426950408889634

LANES = 128
VMEM_LIMIT = 56 * 1024 * 1024


def _params(n_axes):
    return pltpu.CompilerParams(
        dimension_semantics=("arbitrary",) * n_axes, vmem_limit_bytes=VMEM_LIMIT)


def _gelu_exact(x):
    return 0.5 * x * (1.0 + lax.erf(x * (2.0 ** -0.5)))


def _prepare_norm_matmul(x_ref, g_ref, ssq_ref, w_refs, wb_refs, side_refs):
    @pl.when(pl.program_id(1) == 0)
    def _():
        for w_ref, wb_ref in zip(w_refs, wb_refs):
            wb_ref[...] = (w_ref[...] * g_ref[...]).astype(BF16)

    if side_refs:
        src_ref, dst_ref = side_refs
        dst_ref[...] = src_ref[...].astype(BF16)
    ssq = jnp.sum(ssq_ref[...], axis=0)[:, :1]
    return lax.rsqrt(ssq * (1.0 / x_ref.shape[1]) + EPS)


def _side_cast_specs(side, n_j, n_i):
    w_all, layer = side
    K, N = w_all.shape[1:]
    n_steps = n_j * n_i
    rc = K // n_steps
    assert rc * n_steps == K and rc % 16 == 0, (K, n_steps)
    return (pl.BlockSpec((None, rc, N), lambda j, i: (layer, j * n_i + i, 0)),
            pl.BlockSpec((rc, N), lambda j, i: (j * n_i + i, 0)),
            jax.ShapeDtypeStruct((K, N), BF16))


def _mm_norm_kernel(*refs, epilogue, transposed, cfg, has_side):
    x_ref, g_ref, w_ref, ssq_ref = refs[:4]
    wb_ref = refs[-1]
    if has_side:
        extra, side_src = refs[4:-4], refs[-4]
        o_ref, side_dst = refs[-3], refs[-2]
        side_refs = (side_src, side_dst)
    else:
        extra, o_ref, side_refs = refs[4:-2], refs[-2], ()
    r = _prepare_norm_matmul(x_ref, g_ref, ssq_ref, (w_ref,), (wb_ref,), side_refs)
    if transposed:
        acc = lax.dot_general(x_ref[...], wb_ref[...], (((1,), (1,)), ((), ())),
                              preferred_element_type=F32) * r
    else:
        acc = jnp.dot(x_ref[...], wb_ref[...], preferred_element_type=F32) * r
    j = pl.program_id(0)
    tn = o_ref.shape[1]
    if epilogue == "gelu":
        o_ref[...] = _gelu_exact(acc).astype(o_ref.dtype)
    elif epilogue == "rope":
        cos_ref, sin_ref = extra
        dk, k_tile0 = cfg
        half = dk // 2
        scale = jnp.where(j >= k_tile0, dk ** -0.5, 1.0)
        cos = cos_ref[...] * scale
        sin = sin_ref[...] * scale
        for h in range(tn // dk):
            t1 = acc[:, h * dk:h * dk + half]
            t2 = acc[:, h * dk + half:(h + 1) * dk]
            o_ref[:, h * dk:h * dk + half] = (t1 * cos - t2 * sin).astype(o_ref.dtype)
            o_ref[:, h * dk + half:(h + 1) * dk] = (t2 * cos + t1 * sin).astype(o_ref.dtype)
    elif epilogue == "qknorm":
        qg_ref, kg_ref = extra
        hd, k_tile0, q_scale = cfg
        gain = jnp.where(j >= k_tile0, kg_ref[...], qg_ref[...] * q_scale)
        for h in range(tn // hd):
            t = acc[:, h * hd:(h + 1) * hd]
            y = t * lax.rsqrt(jnp.mean(t * t, axis=-1, keepdims=True) + EPS) * gain
            o_ref[:, h * hd:(h + 1) * hd] = y.astype(o_ref.dtype)
    else:
        o_ref[...] = acc.astype(o_ref.dtype)


def _ssq_spec(ssq, tm):
    return pl.BlockSpec((ssq.shape[0], tm, LANES), lambda j, i: (0, i, 0))


def _mm_norm(x, gain, w_all, layer, g_layer, n_out, *, col0=0, transposed=False, epilogue=None,
             extra=(), extra_specs=(), cfg=None, side=None, out_dtype=BF16, tm=1024, tn=1024,
             name="mm_norm"):
    xb, ssq = x
    T, K = xb.shape
    n_j, n_i = n_out // tn, T // tm
    if transposed:
        g_spec = pl.BlockSpec((None, 1, K), lambda j, i: (g_layer, 0, 0))
        w_spec = pl.BlockSpec((None, tn, K), lambda j, i: (layer, col0 + j, 0))
        wb_shape = (tn, K)
    else:
        g_spec = pl.BlockSpec((None, K, 1), lambda j, i: (g_layer, 0, 0))
        w_spec = pl.BlockSpec((None, K, tn), lambda j, i: (layer, 0, col0 + j))
        wb_shape = (K, tn)
    in_specs = [pl.BlockSpec((tm, K), lambda j, i: (i, 0)), g_spec, w_spec, _ssq_spec(ssq, tm),
                *extra_specs]
    operands = [xb, gain, w_all, ssq, *extra]
    out_specs = [pl.BlockSpec((tm, tn), lambda j, i: (i, j))]
    out_shape = [jax.ShapeDtypeStruct((T, n_out), out_dtype)]
    if side is not None:
        src_spec, dst_spec, dst_shape = _side_cast_specs(side, n_j, n_i)
        in_specs.append(src_spec)
        operands.append(side[0])
        out_specs.append(dst_spec)
        out_shape.append(dst_shape)
    out = pl.pallas_call(
        functools.partial(_mm_norm_kernel, epilogue=epilogue, transposed=transposed, cfg=cfg,
                          has_side=side is not None),
        grid=(n_j, n_i),
        in_specs=in_specs,
        out_specs=out_specs,
        out_shape=out_shape,
        scratch_shapes=[pltpu.VMEM(wb_shape, BF16)],
        compiler_params=_params(2),
        name=name,
    )(*operands)
    return out if side is not None else out[0]


def _mm_swiglu_kernel(x_ref, g_ref, wg_ref, wu_ref, ssq_ref, side_src, o_ref, side_dst,
                      wgb_ref, wub_ref):
    r = _prepare_norm_matmul(x_ref, g_ref, ssq_ref, (wg_ref, wu_ref), (wgb_ref, wub_ref),
                             (side_src, side_dst))
    x = x_ref[...]
    gate = jnp.dot(x, wgb_ref[...], preferred_element_type=F32) * r
    up = jnp.dot(x, wub_ref[...], preferred_element_type=F32) * r
    o_ref[...] = (jax.nn.silu(gate) * up).astype(o_ref.dtype)


def _mm_swiglu(x, g_col, wg_all, wu_all, layer, side, *, tm=1024, tn=512):
    xb, ssq = x
    T, K = xb.shape
    F = wg_all.shape[-1]
    n_j, n_i = F // tn, T // tm
    wspec = pl.BlockSpec((None, K, tn), lambda j, i: (layer, 0, j))
    src_spec, dst_spec, dst_shape = _side_cast_specs(side, n_j, n_i)
    return pl.pallas_call(
        _mm_swiglu_kernel,
        grid=(n_j, n_i),
        in_specs=[pl.BlockSpec((tm, K), lambda j, i: (i, 0)),
                  pl.BlockSpec((None, K, 1), lambda j, i: (layer, 0, 0)),
                  wspec, wspec, _ssq_spec(ssq, tm), src_spec],
        out_specs=[pl.BlockSpec((tm, tn), lambda j, i: (i, j)), dst_spec],
        out_shape=[jax.ShapeDtypeStruct((T, F), BF16), dst_shape],
        scratch_shapes=[pltpu.VMEM((K, tn), BF16), pltpu.VMEM((K, tn), BF16)],
        compiler_params=_params(2),
        name="mm_swiglu",
    )(xb, g_col, wg_all, wu_all, ssq, side[0])


def _row_ssq(x):
    return jnp.broadcast_to(jnp.sum(x * x, axis=-1, keepdims=True), (x.shape[0], LANES))


def _cast_stats_kernel(x_ref, xb_ref, ssq_ref):
    x = x_ref[...]
    xb_ref[...] = x.astype(BF16)
    ssq_ref[...] = _row_ssq(x)


def _cast_stats(x, tm=512):
    T, D = x.shape
    return pl.pallas_call(
        _cast_stats_kernel,
        grid=(T // tm,),
        in_specs=[pl.BlockSpec((tm, D), lambda i: (i, 0))],
        out_specs=[pl.BlockSpec((tm, D), lambda i: (i, 0)),
                   pl.BlockSpec((None, tm, LANES), lambda i: (0, i, 0))],
        out_shape=[jax.ShapeDtypeStruct((T, D), BF16), jax.ShapeDtypeStruct((1, T, LANES), F32)],
        compiler_params=_params(1),
        name="cast_stats",
    )(x)


def _mm_residual_kernel(a_ref, w_ref, r_ref, o_ref, ob_ref, ssq_ref):
    x_new = r_ref[...] + jnp.dot(a_ref[...], w_ref[...], preferred_element_type=F32)
    o_ref[...] = x_new
    ob_ref[...] = x_new.astype(BF16)
    ssq_ref[...] = _row_ssq(x_new)


def _residual_tiles(K, N):
    return 512, (N if K <= 2048 else N // 2)


def _mm_residual(a, wb, res):
    T, K = a.shape
    N = wb.shape[-1]
    tm, tn = _residual_tiles(K, N)
    tile = pl.BlockSpec((tm, tn), lambda j, i: (i, j))
    x_new, xb, ssq = pl.pallas_call(
        _mm_residual_kernel,
        grid=(N // tn, T // tm),
        in_specs=[pl.BlockSpec((tm, K), lambda j, i: (i, 0)),
                  pl.BlockSpec((K, tn), lambda j, i: (0, j)),
                  tile],
        out_specs=[tile, tile, pl.BlockSpec((None, tm, LANES), lambda j, i: (j, i, 0))],
        out_shape=[jax.ShapeDtypeStruct((T, N), F32), jax.ShapeDtypeStruct((T, N), BF16),
                   jax.ShapeDtypeStruct((N // tn, T, LANES), F32)],
        compiler_params=_params(2),
        name="mm_residual",
    )(a, wb, res)
    return x_new, (xb, ssq)


def _rope_kernel(pos_ref, invf_ref, cos_ref, sin_ref):
    ang = pos_ref[...].astype(F32) * invf_ref[...]
    cos_ref[...] = jnp.cos(ang)
    sin_ref[...] = jnp.sin(ang)


def _rope_tables(pos_col, inv_freq, tm=512):
    T = pos_col.shape[0]
    half = inv_freq.shape[-1]
    out = jax.ShapeDtypeStruct((T, half), F32)
    return pl.pallas_call(
        _rope_kernel,
        grid=(T // tm,),
        in_specs=[pl.BlockSpec((tm, 1), lambda i: (i, 0)),
                  pl.BlockSpec((1, half), lambda i: (0, 0))],
        out_specs=[pl.BlockSpec((tm, half), lambda i: (i, 0))] * 2,
        out_shape=[out, out],
        compiler_params=_params(1),
        name="rope_tables",
    )(pos_col, inv_freq)


def _retention_kernel(qk_ref, vg_ref, din_ref, qd_ref, kd_ref, cd_ref, gn_ref,
                      y_ref, state_ref, *, heads, dk, dv):
    @pl.when(pl.program_id(1) == 0)
    def _():
        state_ref[...] = jnp.zeros_like(state_ref)

    k_off, g_off = heads * dk, heads * dv
    for h in range(heads):
        qb = qk_ref[:, h * dk:(h + 1) * dk]
        kb = qk_ref[:, k_off + h * dk:k_off + (h + 1) * dk]
        v = vg_ref[:, h * dv:(h + 1) * dv]
        g = vg_ref[:, g_off + h * dv:g_off + (h + 1) * dv].astype(F32)
        scores = lax.dot_general(qb, kb, (((1,), (1,)), ((), ())),
                                 preferred_element_type=F32) * din_ref[h]
        inner = jnp.dot(scores.astype(BF16), v, preferred_element_type=F32)
        state = state_ref[h]
        cross = jnp.dot(qb, state.astype(BF16), preferred_element_type=F32) * qd_ref[h]
        k_dec = (kb.astype(F32) * kd_ref[h]).astype(BF16)
        state_ref[h] = state * cd_ref[h] + lax.dot_general(
            k_dec, v, (((0,), (0,)), ((), ())), preferred_element_type=F32)
        o = inner + cross
        oc = o - jnp.mean(o, axis=-1, keepdims=True)
        var = jnp.mean(oc * oc, axis=-1, keepdims=True)
        on = oc * lax.rsqrt(var + EPS) * gn_ref[:, h * dv:(h + 1) * dv]
        y_ref[:, h * dv:(h + 1) * dv] = (jax.nn.silu(g) * on).astype(y_ref.dtype)


def _retention(qk, vg, gn_all, layer, batch, seq):
    T = qk.shape[0]
    H, C = RET_HEADS, RET_BLOCK
    dk = qk.shape[1] // (2 * H)
    dv = vg.shape[1] // (2 * H)
    nc = seq // C
    log_gamma = jnp.log1p(-jnp.exp2(-5.0 - jnp.arange(H, dtype=F32)))
    idx = jnp.arange(C, dtype=F32)
    dist = idx[:, None] - idx[None, :]
    decay_in = jnp.where(dist >= 0,
                         jnp.exp(jnp.maximum(dist, 0.0)[None] * log_gamma[:, None, None]), 0.0)
    q_decay = jnp.exp((idx + 1.0)[None, :] * log_gamma[:, None])[..., None]
    k_decay = jnp.exp((C - 1.0 - idx)[None, :] * log_gamma[:, None])[..., None]
    chunk_decay = jnp.exp(C * log_gamma)[:, None, None]

    def whole(shape):
        return pl.BlockSpec(shape, lambda b, c: (0,) * len(shape))

    return pl.pallas_call(
        functools.partial(_retention_kernel, heads=H, dk=dk, dv=dv),
        grid=(batch, nc),
        in_specs=[pl.BlockSpec((C, 2 * H * dk), lambda b, c: (b * nc + c, 0)),
                  pl.BlockSpec((C, 2 * H * dv), lambda b, c: (b * nc + c, 0)),
                  whole((H, C, C)), whole((H, C, 1)), whole((H, C, 1)), whole((H, 1, 1)),
                  pl.BlockSpec((None, 1, H * dv), lambda b, c: (layer, 0, 0))],
        out_specs=pl.BlockSpec((C, H * dv), lambda b, c: (b * nc + c, 0)),
        out_shape=jax.ShapeDtypeStruct((T, H * dv), BF16),
        scratch_shapes=[pltpu.VMEM((H, dk, dv), F32)],
        compiler_params=_params(2),
        name="retention",
    )(qk, vg, decay_in, q_decay, k_decay, chunk_decay, gn_all)


def _sgu_kernel(u_ref, v_ref, lng_ref, lnb_ref, ws_ref, bs_ref, y_ref, *, groups):
    v = v_ref[...].astype(F32)
    vc = v - jnp.mean(v, axis=-1, keepdims=True)
    var = jnp.mean(vc * vc, axis=-1, keepdims=True)
    vn = (vc * lax.rsqrt(var + EPS) * lng_ref[...] + lnb_ref[...]).astype(BF16)
    C = v.shape[0]
    gw = v.shape[1] // groups
    row = lax.broadcasted_iota(jnp.int32, (C, C), 0)
    col = lax.broadcasted_iota(jnp.int32, (C, C), 1)
    for g in range(groups):
        w = jnp.where(col <= row, ws_ref[g], 0.0).astype(BF16)
        mixed = jnp.dot(w, vn[:, g * gw:(g + 1) * gw], preferred_element_type=F32)
        mixed = mixed + bs_ref[:, g:g + 1]
        u = u_ref[:, g * gw:(g + 1) * gw].astype(F32)
        y_ref[:, g * gw:(g + 1) * gw] = (u * mixed).astype(y_ref.dtype)


def _sgu(z, lng_all, lnb_all, ws_all, bs_t, layer):
    T, dffn = z.shape
    half = dffn // 2
    G, C = GMLP_GROUPS, CHUNK
    vec = pl.BlockSpec((None, 1, half), lambda i: (layer, 0, 0))
    return pl.pallas_call(
        functools.partial(_sgu_kernel, groups=G),
        grid=(T // C,),
        in_specs=[pl.BlockSpec((C, half), lambda i: (i, 0)),
                  pl.BlockSpec((C, half), lambda i: (i, 1)),
                  vec, vec,
                  pl.BlockSpec((None, G, C, C), lambda i: (layer, 0, 0, 0)),
                  pl.BlockSpec((None, C, G), lambda i: (layer, 0, 0))],
        out_specs=pl.BlockSpec((C, half), lambda i: (i, 0)),
        out_shape=jax.ShapeDtypeStruct((T, half), BF16),
        compiler_params=_params(1),
        name="sgu",
    )(z, z, lng_all, lnb_all, ws_all, bs_t)


def _fox_prep_kernel(f_ref, bf_ref, aqk_ref, carry_ref, *, hd):
    @pl.when(pl.program_id(1) == 0)
    def _():
        carry_ref[...] = jnp.zeros_like(carry_ref)

    log_f = jax.nn.log_sigmoid(f_ref[...] + bf_ref[...])
    tb, heads = log_f.shape
    row = lax.broadcasted_iota(jnp.int32, (tb, tb), 0)
    col = lax.broadcasted_iota(jnp.int32, (tb, tb), 1)
    tri = jnp.where(col <= row, 1.0, 0.0).astype(BF16)

    def split3(x):
        hi = x.astype(BF16)
        r1 = x - hi.astype(F32)
        mid = r1.astype(BF16)
        lo = (r1 - mid.astype(F32)).astype(BF16)
        return hi, mid, lo

    c = carry_ref[...] + sum(jnp.dot(tri, part, preferred_element_type=F32)
                             for part in split3(log_f))
    carry_ref[...] = c[tb - 1:tb, :]
    hi, mid, lo = (part.astype(F32) for part in split3(c * LOG2E))
    lane = lax.broadcasted_iota(jnp.int32, (tb, hd), 1)
    for h in range(heads):
        hh, mh, lh = hi[:, h:h + 1], mid[:, h:h + 1], lo[:, h:h + 1]
        aq = jnp.where(lane == 0, hh, jnp.where(lane == 1, mh, jnp.where(
            lane == 2, lh, jnp.where(lane < 6, 1.0, 0.0))))
        ak = jnp.where(lane < 3, 1.0, jnp.where(lane == 3, -hh, jnp.where(
            lane == 4, -mh, jnp.where(lane == 5, -lh, 0.0))))
        aqk_ref[:, h * hd:(h + 1) * hd] = aq.astype(BF16)
        aqk_ref[:, (heads + h) * hd:(heads + h + 1) * hd] = ak.astype(BF16)


def _fox_prep(f_logit, bf_all, layer, hd, batch, seq, tb=512):
    T, H = f_logit.shape
    nb = seq // tb
    return pl.pallas_call(
        functools.partial(_fox_prep_kernel, hd=hd),
        grid=(batch, nb),
        in_specs=[pl.BlockSpec((tb, H), lambda b, s: (b * nb + s, 0)),
                  pl.BlockSpec((None, 1, H), lambda b, s: (layer, 0, 0))],
        out_specs=pl.BlockSpec((tb, 2 * H * hd), lambda b, s: (b * nb + s, 0)),
        out_shape=jax.ShapeDtypeStruct((T, 2 * H * hd), BF16),
        scratch_shapes=[pltpu.VMEM((1, H), F32)],
        compiler_params=_params(2),
        name="fox_prep",
    )(f_logit, bf_all)


def _fox_attn_kernel(qi_tab, ki_tab, q_ref, aq_ref, k_ref, ak_ref, v_ref, g_ref, y_ref,
                     m_ref, acc_ref, *, hg, hd):
    p = pl.program_id(2)
    qi = qi_tab[p]
    ki = ki_tab[p]

    @pl.when(ki == 0)
    def _():
        m_ref[...] = jnp.full_like(m_ref, -jnp.inf)
        acc_ref[...] = jnp.zeros_like(acc_ref)

    def step(masked):
        tk = k_ref.shape[0]
        ones_col = jnp.where(lax.broadcasted_iota(jnp.int32, (tk, hd), 1) == 0,
                             1.0, 0.0).astype(BF16)
        for h in range(hg):
            sl = slice(h * hd, (h + 1) * hd)
            qc = jnp.concatenate([q_ref[:, sl], aq_ref[:, sl]], axis=1)
            kc = jnp.concatenate([k_ref[:, sl], ak_ref[:, sl]], axis=1)
            s = lax.dot_general(qc, kc, (((1,), (1,)), ((), ())),
                                preferred_element_type=F32)
            if masked:
                row = lax.broadcasted_iota(jnp.int32, s.shape, 0)
                col = lax.broadcasted_iota(jnp.int32, s.shape, 1)
                s = jnp.where(col <= row, s, -jnp.inf)
            m_prev = m_ref[h]
            m_new = jnp.maximum(m_prev, jnp.max(s, axis=-1, keepdims=True))
            alpha = jnp.exp2(m_prev - m_new)
            pexp = jnp.exp2(s - m_new[:, :1]).astype(BF16)
            vc = jnp.concatenate([v_ref[:, sl], ones_col], axis=1)
            pv = jnp.dot(pexp, vc, preferred_element_type=F32)
            acc_ref[h] = jnp.concatenate([alpha, alpha], axis=1) * acc_ref[h] + pv
            m_ref[h] = m_new

    @pl.when(ki < qi)
    def _():
        step(False)

    @pl.when(ki == qi)
    def _():
        step(True)
        for h in range(hg):
            sl = slice(h * hd, (h + 1) * hd)
            acc = acc_ref[h]
            o = acc[:, :hd] / acc[:, hd:hd + 1]
            y_ref[:, sl] = (jax.nn.sigmoid(g_ref[:, sl].astype(F32)) * o).astype(y_ref.dtype)


def _fox_attention(qkn, aqk, vg, batch, seq, tq=512, hg=FOX_HEADS):
    T = qkn.shape[0]
    H = FOX_HEADS
    hd = qkn.shape[1] // (2 * H)
    nq = seq // tq
    ng = H // hg
    pairs = [(qi, ki) for qi in range(nq) for ki in range(qi + 1)]
    qi_tab = jnp.asarray([p[0] for p in pairs], jnp.int32)
    ki_tab = jnp.asarray([p[1] for p in pairs], jnp.int32)

    def q_side(col0):
        return pl.BlockSpec((tq, hg * hd), lambda b, g, p, qt, kt: (b * nq + qt[p], col0 + g))

    def k_side(col0):
        return pl.BlockSpec((tq, hg * hd), lambda b, g, p, qt, kt: (b * nq + kt[p], col0 + g))

    grid_spec = pltpu.PrefetchScalarGridSpec(
        num_scalar_prefetch=2,
        grid=(batch, ng, len(pairs)),
        in_specs=[q_side(0), q_side(0), k_side(ng), k_side(ng), k_side(0), q_side(ng)],
        out_specs=q_side(0),
        scratch_shapes=[pltpu.VMEM((hg, tq, LANES), F32), pltpu.VMEM((hg, tq, 2 * hd), F32)],
    )
    return pl.pallas_call(
        functools.partial(_fox_attn_kernel, hg=hg, hd=hd),
        grid_spec=grid_spec,
        out_shape=jax.ShapeDtypeStruct((T, H * hd), BF16),
        compiler_params=_params(3),
        name="fox_attention",
    )(qi_tab, ki_tab, qkn, aqk, qkn, aqk, vg, vg)


def kernel(x, positions, mix_norm_g, ffn_norm_g, ret_w_in, ret_gn_g, ret_w_out, gmlp_w_in, gmlp_ln_g, gmlp_ln_b, gmlp_w_s, gmlp_b_s, gmlp_w_out, fox_w_in, fox_b_f, fox_qn_g, fox_kn_g, fox_w_out, ffn_w_gate, ffn_w_up, ffn_w_down):
    B, S, D = x.shape
    T = B * S
    depth = mix_norm_g.shape[0]
    xf = x.reshape(T, D)
    xn = _cast_stats(xf)

    def rows(a):
        return a.reshape(a.shape[0], 1, a.shape[1])

    mix_g, ffn_g = mix_norm_g[..., None], ffn_norm_g[..., None]
    mix_g_row = rows(mix_norm_g)
    ret_gn, lng, lnb = rows(ret_gn_g), rows(gmlp_ln_g), rows(gmlp_ln_b)
    qn_g, kn_g, fox_bf = rows(fox_qn_g), rows(fox_kn_g), rows(fox_b_f)
    bs_t = jnp.swapaxes(gmlp_b_s, 1, 2)
    fox_w_t = jnp.swapaxes(fox_w_in, 1, 2)

    tm = tn = 1024
    ret_dk = ret_w_in.shape[-1] // (6 * RET_HEADS)
    ret_qk = 2 * RET_HEADS * ret_dk
    inv_freq = (ROPE_BASE ** (-jnp.arange(ret_dk // 2, dtype=F32) / (ret_dk // 2))).reshape(1, -1)
    cos, sin = _rope_tables(positions.reshape(T, 1), inv_freq)
    rope_spec = pl.BlockSpec((tm, ret_dk // 2), lambda j, i: (i, 0))

    hd = fox_qn_g.shape[-1]
    fox_qk = 2 * FOX_HEADS * hd
    for i in range(depth):
        kind, j = i % N_MIXERS, i // N_MIXERS
        if kind == 0:
            qk = _mm_norm(xn, mix_g, ret_w_in, j, i, ret_qk, epilogue="rope", extra=(cos, sin),
                          extra_specs=(rope_spec, rope_spec), cfg=(ret_dk, ret_qk // 2 // tn),
                          tm=tm, tn=tn, name="mm_ret_qk")
            vg, w_out = _mm_norm(xn, mix_g, ret_w_in, j, i, ret_w_in.shape[-1] - ret_qk,
                                 col0=ret_qk // tn, side=(ret_w_out, j), tm=tm, tn=tn,
                                 name="mm_ret_vg")
            y = _retention(qk, vg, ret_gn, j, B, S)
        elif kind == 1:
            z, w_out = _mm_norm(xn, mix_g, gmlp_w_in, j, i, gmlp_w_in.shape[-1], epilogue="gelu",
                                side=(gmlp_w_out, j), name="mm_gmlp_in")
            y = _sgu(z, lng, lnb, gmlp_w_s, bs_t, j)
        else:
            gain_spec = pl.BlockSpec((None, 1, hd), lambda jj, ii: (j, 0, 0))
            qkn = _mm_norm(xn, mix_g_row, fox_w_t, j, i, fox_qk, transposed=True,
                           epilogue="qknorm", extra=(qn_g, kn_g), extra_specs=(gain_spec, gain_spec),
                           cfg=(hd, fox_qk // 2 // tn, hd ** -0.5 * LOG2E),
                           tm=tm, tn=tn, name="mm_fox_qk")
            vg, w_out = _mm_norm(xn, mix_g_row, fox_w_t, j, i, fox_qk, col0=fox_qk // tn,
                                 transposed=True, side=(fox_w_out, j), tm=tm, tn=tn,
                                 name="mm_fox_vg")
            f_logit = _mm_norm(xn, mix_g_row, fox_w_t, j, i, FOX_HEADS,
                               col0=2 * fox_qk // FOX_HEADS, transposed=True, out_dtype=F32,
                               tm=tm, tn=FOX_HEADS, name="mm_fox_forget")
            aqk = _fox_prep(f_logit, fox_bf, j, hd, B, S)
            y = _fox_attention(qkn, aqk, vg, B, S)
        xf, xn = _mm_residual(y, w_out, xf)
        a, w_down = _mm_swiglu(xn, ffn_g, ffn_w_gate, ffn_w_up, i, (ffn_w_down, i))
        xf, xn = _mm_residual(a, w_down, xf)
    return xf.reshape(B, S, D)
```

```python
import functools

import jax
import jax.numpy as jnp
from jax import lax
from jax.experimental import pallas as pl
from jax.experimental.pallas import tpu as pltpu

F32 = jnp.float32
BF16 = jnp.bfloat16

EPS = 1e-6
CHUNK = 128
RET_BLOCK = 256
NORM_ROWS = 32
ROPE_BASE = 10000.0
RET_HEADS = 8
GMLP_GROUPS = 8
FOX_HEADS = 16
N_MIXERS = 3
LOG2E = 1.4426950408889634

LANES = 128
VMEM_LIMIT = 56 * 1024 * 1024


def _params(n_axes):
    return pltpu.CompilerParams(
        dimension_semantics=("arbitrary",) * n_axes, vmem_limit_bytes=VMEM_LIMIT)


def _gelu_exact(x):
    return 0.5 * x * (1.0 + lax.erf(x * (2.0 ** -0.5)))


def _prepare_norm_matmul(x_ref, g_ref, ssq_ref, w_refs, wb_refs, side_refs):
    @pl.when(pl.program_id(1) == 0)
    def _():
        for w_ref, wb_ref in zip(w_refs, wb_refs):
            wb_ref[...] = (w_ref[...] * g_ref[...]).astype(BF16)

    if side_refs:
        src_ref, dst_ref = side_refs
        dst_ref[...] = src_ref[...].astype(BF16)
    ssq = jnp.sum(ssq_ref[...], axis=0)[:, :1]
    return lax.rsqrt(ssq * (1.0 / x_ref.shape[1]) + EPS)


def _side_cast_specs(side, n_j, n_i):
    w_all, layer = side
    K, N = w_all.shape[1:]
    n_steps = n_j * n_i
    rc = K // n_steps
    assert rc * n_steps == K and rc % 16 == 0, (K, n_steps)
    return (pl.BlockSpec((None, rc, N), lambda j, i: (layer, j * n_i + i, 0)),
            pl.BlockSpec((rc, N), lambda j, i: (j * n_i + i, 0)),
            jax.ShapeDtypeStruct((K, N), BF16))


def _mm_norm_kernel(*refs, epilogue, transposed, cfg, has_side):
    x_ref, g_ref, w_ref, ssq_ref = refs[:4]
    wb_ref = refs[-1]
    if has_side:
        extra, side_src = refs[4:-4], refs[-4]
        o_ref, side_dst = refs[-3], refs[-2]
        side_refs = (side_src, side_dst)
    else:
        extra, o_ref, side_refs = refs[4:-2], refs[-2], ()
    r = _prepare_norm_matmul(x_ref, g_ref, ssq_ref, (w_ref,), (wb_ref,), side_refs)
    if transposed:
        acc = lax.dot_general(x_ref[...], wb_ref[...], (((1,), (1,)), ((), ())),
                              preferred_element_type=F32) * r
    else:
        acc = jnp.dot(x_ref[...], wb_ref[...], preferred_element_type=F32) * r
    j = pl.program_id(0)
    tn = o_ref.shape[1]
    if epilogue == "gelu":
        o_ref[...] = _gelu_exact(acc).astype(o_ref.dtype)
    elif epilogue == "rope":
        cos_ref, sin_ref = extra
        dk, k_tile0 = cfg
        half = dk // 2
        scale = jnp.where(j >= k_tile0, dk ** -0.5, 1.0)
        cos = cos_ref[...] * scale
        sin = sin_ref[...] * scale
        for h in range(tn // dk):
            t1 = acc[:, h * dk:h * dk + half]
            t2 = acc[:, h * dk + half:(h + 1) * dk]
            o_ref[:, h * dk:h * dk + half] = (t1 * cos - t2 * sin).astype(o_ref.dtype)
            o_ref[:, h * dk + half:(h + 1) * dk] = (t2 * cos + t1 * sin).astype(o_ref.dtype)
    elif epilogue in ("fox_q", "fox_k", "fox_v"):
        hd = cfg[0]
        heads = acc.shape[1] // hd
        lane = lax.broadcasted_iota(jnp.int32, (acc.shape[0], hd), 1)
        if epilogue == "fox_v":
            ones_col = jnp.where(lane == 0, 1.0, 0.0).astype(o_ref.dtype)
        else:
            gain_ref, c3_ref = extra
            gain = gain_ref[...] * cfg[1]
        for h in range(heads):
            t = acc[:, h * hd:(h + 1) * hd]
            if epilogue == "fox_v":
                data, aug = t, ones_col
            else:
                data = t * lax.rsqrt(jnp.mean(t * t, axis=-1, keepdims=True) + EPS) * gain
                hi, mid, lo = (c3_ref[:, p * heads + h:p * heads + h + 1] for p in range(3))
                if epilogue == "fox_q":
                    aug = jnp.where(lane == 0, hi, jnp.where(lane == 1, mid, jnp.where(
                        lane == 2, lo, jnp.where(lane < 6, 1.0, 0.0))))
                else:
                    aug = jnp.where(lane < 3, 1.0, jnp.where(lane == 3, -hi, jnp.where(
                        lane == 4, -mid, jnp.where(lane == 5, -lo, 0.0))))
            o_ref[:, 2 * h * hd:(2 * h + 1) * hd] = data.astype(o_ref.dtype)
            o_ref[:, (2 * h + 1) * hd:(2 * h + 2) * hd] = aug.astype(o_ref.dtype)
    else:
        o_ref[...] = acc.astype(o_ref.dtype)


def _ssq_spec(ssq, tm):
    return pl.BlockSpec((ssq.shape[0], tm, LANES), lambda j, i: (0, i, 0))


def _mm_norm(x, gain, w_all, layer, g_layer, n_out, *, col0=0, transposed=False, epilogue=None,
             extra=(), extra_specs=(), cfg=None, side=None, out_dtype=BF16, tm=1024, tn=1024,
             name="mm_norm"):
    xb, ssq = x
    T, K = xb.shape
    n_j, n_i = n_out // tn, T // tm
    if transposed:
        g_spec = pl.BlockSpec((None, 1, K), lambda j, i: (g_layer, 0, 0))
        w_spec = pl.BlockSpec((None, tn, K), lambda j, i: (layer, col0 + j, 0))
        wb_shape = (tn, K)
    else:
        g_spec = pl.BlockSpec((None, K, 1), lambda j, i: (g_layer, 0, 0))
        w_spec = pl.BlockSpec((None, K, tn), lambda j, i: (layer, 0, col0 + j))
        wb_shape = (K, tn)
    in_specs = [pl.BlockSpec((tm, K), lambda j, i: (i, 0)), g_spec, w_spec, _ssq_spec(ssq, tm),
                *extra_specs]
    operands = [xb, gain, w_all, ssq, *extra]
    widen = 2 if epilogue in ("fox_q", "fox_k", "fox_v") else 1
    out_specs = [pl.BlockSpec((tm, widen * tn), lambda j, i: (i, j))]
    out_shape = [jax.ShapeDtypeStruct((T, widen * n_out), out_dtype)]
    if side is not None:
        src_spec, dst_spec, dst_shape = _side_cast_specs(side, n_j, n_i)
        in_specs.append(src_spec)
        operands.append(side[0])
        out_specs.append(dst_spec)
        out_shape.append(dst_shape)
    out = pl.pallas_call(
        functools.partial(_mm_norm_kernel, epilogue=epilogue, transposed=transposed, cfg=cfg,
                          has_side=side is not None),
        grid=(n_j, n_i),
        in_specs=in_specs,
        out_specs=out_specs,
        out_shape=out_shape,
        scratch_shapes=[pltpu.VMEM(wb_shape, BF16)],
        compiler_params=_params(2),
        name=name,
    )(*operands)
    return out if side is not None else out[0]


def _mm_swiglu_kernel(x_ref, g_ref, wg_ref, wu_ref, ssq_ref, side_src, o_ref, side_dst,
                      wgb_ref, wub_ref):
    r = _prepare_norm_matmul(x_ref, g_ref, ssq_ref, (wg_ref, wu_ref), (wgb_ref, wub_ref),
                             (side_src, side_dst))
    x = x_ref[...]
    gate = jnp.dot(x, wgb_ref[...], preferred_element_type=F32) * r
    up = jnp.dot(x, wub_ref[...], preferred_element_type=F32) * r
    o_ref[...] = (jax.nn.silu(gate) * up).astype(o_ref.dtype)


def _mm_swiglu(x, g_col, wg_all, wu_all, layer, side, *, tm=1024, tn=512):
    xb, ssq = x
    T, K = xb.shape
    F = wg_all.shape[-1]
    n_j, n_i = F // tn, T // tm
    wspec = pl.BlockSpec((None, K, tn), lambda j, i: (layer, 0, j))
    src_spec, dst_spec, dst_shape = _side_cast_specs(side, n_j, n_i)
    return pl.pallas_call(
        _mm_swiglu_kernel,
        grid=(n_j, n_i),
        in_specs=[pl.BlockSpec((tm, K), lambda j, i: (i, 0)),
                  pl.BlockSpec((None, K, 1), lambda j, i: (layer, 0, 0)),
                  wspec, wspec, _ssq_spec(ssq, tm), src_spec],
        out_specs=[pl.BlockSpec((tm, tn), lambda j, i: (i, j)), dst_spec],
        out_shape=[jax.ShapeDtypeStruct((T, F), BF16), dst_shape],
        scratch_shapes=[pltpu.VMEM((K, tn), BF16), pltpu.VMEM((K, tn), BF16)],
        compiler_params=_params(2),
        name="mm_swiglu",
    )(xb, g_col, wg_all, wu_all, ssq, side[0])


def _row_ssq(x):
    return jnp.broadcast_to(jnp.sum(x * x, axis=-1, keepdims=True), (x.shape[0], LANES))


def _cast_stats_kernel(x_ref, xb_ref, ssq_ref):
    x = x_ref[...]
    xb_ref[...] = x.astype(BF16)
    ssq_ref[...] = _row_ssq(x)


def _cast_stats(x, tm=512):
    T, D = x.shape
    return pl.pallas_call(
        _cast_stats_kernel,
        grid=(T // tm,),
        in_specs=[pl.BlockSpec((tm, D), lambda i: (i, 0))],
        out_specs=[pl.BlockSpec((tm, D), lambda i: (i, 0)),
                   pl.BlockSpec((None, tm, LANES), lambda i: (0, i, 0))],
        out_shape=[jax.ShapeDtypeStruct((T, D), BF16), jax.ShapeDtypeStruct((1, T, LANES), F32)],
        compiler_params=_params(1),
        name="cast_stats",
    )(x)


def _mm_residual_kernel(a_ref, w_ref, r_ref, o_ref, ob_ref, ssq_ref):
    x_new = r_ref[...] + jnp.dot(a_ref[...], w_ref[...], preferred_element_type=F32)
    o_ref[...] = x_new
    ob_ref[...] = x_new.astype(BF16)
    ssq_ref[...] = _row_ssq(x_new)


def _residual_tiles(K, N):
    return 512, (N if K <= 2048 else N // 2)


def _mm_residual(a, wb, res):
    T, K = a.shape
    N = wb.shape[-1]
    tm, tn = _residual_tiles(K, N)
    tile = pl.BlockSpec((tm, tn), lambda j, i: (i, j))
    x_new, xb, ssq = pl.pallas_call(
        _mm_residual_kernel,
        grid=(N // tn, T // tm),
        in_specs=[pl.BlockSpec((tm, K), lambda j, i: (i, 0)),
                  pl.BlockSpec((K, tn), lambda j, i: (0, j)),
                  tile],
        out_specs=[tile, tile, pl.BlockSpec((None, tm, LANES), lambda j, i: (j, i, 0))],
        out_shape=[jax.ShapeDtypeStruct((T, N), F32), jax.ShapeDtypeStruct((T, N), BF16),
                   jax.ShapeDtypeStruct((N // tn, T, LANES), F32)],
        compiler_params=_params(2),
        name="mm_residual",
    )(a, wb, res)
    return x_new, (xb, ssq)


def _rope_kernel(pos_ref, invf_ref, cos_ref, sin_ref):
    ang = pos_ref[...].astype(F32) * invf_ref[...]
    cos_ref[...] = jnp.cos(ang)
    sin_ref[...] = jnp.sin(ang)


def _rope_tables(pos_col, inv_freq, tm=512):
    T = pos_col.shape[0]
    half = inv_freq.shape[-1]
    out = jax.ShapeDtypeStruct((T, half), F32)
    return pl.pallas_call(
        _rope_kernel,
        grid=(T // tm,),
        in_specs=[pl.BlockSpec((tm, 1), lambda i: (i, 0)),
                  pl.BlockSpec((1, half), lambda i: (0, 0))],
        out_specs=[pl.BlockSpec((tm, half), lambda i: (i, 0))] * 2,
        out_shape=[out, out],
        compiler_params=_params(1),
        name="rope_tables",
    )(pos_col, inv_freq)


def _retention_kernel(qk_ref, vg_ref, din_ref, qd_ref, kd_ref, cd_ref, gn_ref,
                      y_ref, state_ref, *, heads, dk, dv):
    @pl.when(pl.program_id(1) == 0)
    def _():
        state_ref[...] = jnp.zeros_like(state_ref)

    k_off, g_off = heads * dk, heads * dv
    rows = qk_ref.shape[0]
    for h in range(heads):
        qb = qk_ref[:, h * dk:(h + 1) * dk]
        kb = qk_ref[:, k_off + h * dk:k_off + (h + 1) * dk]
        v = vg_ref[:, h * dv:(h + 1) * dv]
        scores = lax.dot_general(qb, kb, (((1,), (1,)), ((), ())),
                                 preferred_element_type=F32) * din_ref[h]
        state = state_ref[h]
        q_dec = (qb.astype(F32) * qd_ref[h]).astype(BF16)
        o = (jnp.dot(scores.astype(BF16), v, preferred_element_type=F32)
             + jnp.dot(q_dec, state.astype(BF16), preferred_element_type=F32))
        k_dec = (kb.astype(F32) * kd_ref[h]).astype(BF16)
        state_ref[h] = state * cd_ref[h] + lax.dot_general(
            k_dec, v, (((0,), (0,)), ((), ())), preferred_element_type=F32)
        gain = gn_ref[:, h * dv:(h + 1) * dv]
        for r0 in range(0, rows, NORM_ROWS):
            strip = o[r0:r0 + NORM_ROWS]
            sc = strip - jnp.mean(strip, axis=-1, keepdims=True)
            var = jnp.mean(sc * sc, axis=-1, keepdims=True)
            g = vg_ref[r0:r0 + NORM_ROWS, g_off + h * dv:g_off + (h + 1) * dv].astype(F32)
            y_ref[r0:r0 + NORM_ROWS, h * dv:(h + 1) * dv] = (
                jax.nn.silu(g) * (sc * lax.rsqrt(var + EPS) * gain)).astype(y_ref.dtype)


def _retention(qk, vg, gn_all, layer, batch, seq):
    T = qk.shape[0]
    H, C = RET_HEADS, RET_BLOCK
    dk = qk.shape[1] // (2 * H)
    dv = vg.shape[1] // (2 * H)
    nc = seq // C
    log_gamma = jnp.log1p(-jnp.exp2(-5.0 - jnp.arange(H, dtype=F32)))
    idx = jnp.arange(C, dtype=F32)
    dist = idx[:, None] - idx[None, :]
    decay_in = jnp.where(dist >= 0,
                         jnp.exp(jnp.maximum(dist, 0.0)[None] * log_gamma[:, None, None]), 0.0)
    q_decay = jnp.exp((idx + 1.0)[None, :] * log_gamma[:, None])[..., None]
    k_decay = jnp.exp((C - 1.0 - idx)[None, :] * log_gamma[:, None])[..., None]
    chunk_decay = jnp.exp(C * log_gamma)[:, None, None]

    def whole(shape):
        return pl.BlockSpec(shape, lambda b, c: (0,) * len(shape))

    return pl.pallas_call(
        functools.partial(_retention_kernel, heads=H, dk=dk, dv=dv),
        grid=(batch, nc),
        in_specs=[pl.BlockSpec((C, 2 * H * dk), lambda b, c: (b * nc + c, 0)),
                  pl.BlockSpec((C, 2 * H * dv), lambda b, c: (b * nc + c, 0)),
                  whole((H, C, C)), whole((H, C, 1)), whole((H, C, 1)), whole((H, 1, 1)),
                  pl.BlockSpec((None, 1, H * dv), lambda b, c: (layer, 0, 0))],
        out_specs=pl.BlockSpec((C, H * dv), lambda b, c: (b * nc + c, 0)),
        out_shape=jax.ShapeDtypeStruct((T, H * dv), BF16),
        scratch_shapes=[pltpu.VMEM((H, dk, dv), F32)],
        compiler_params=_params(2),
        name="retention",
    )(qk, vg, decay_in, q_decay, k_decay, chunk_decay, gn_all)


def _sgu_kernel(u_ref, v_ref, lng_ref, lnb_ref, ws_ref, bs_ref, y_ref, *, groups):
    v = v_ref[...].astype(F32)
    vc = v - jnp.mean(v, axis=-1, keepdims=True)
    var = jnp.mean(vc * vc, axis=-1, keepdims=True)
    vn = (vc * lax.rsqrt(var + EPS) * lng_ref[...] + lnb_ref[...]).astype(BF16)
    C = v.shape[0]
    gw = v.shape[1] // groups
    row = lax.broadcasted_iota(jnp.int32, (C, C), 0)
    col = lax.broadcasted_iota(jnp.int32, (C, C), 1)
    for g in range(groups):
        w = jnp.where(col <= row, ws_ref[g], 0.0).astype(BF16)
        mixed = jnp.dot(w, vn[:, g * gw:(g + 1) * gw], preferred_element_type=F32)
        mixed = mixed + bs_ref[:, g:g + 1]
        u = u_ref[:, g * gw:(g + 1) * gw].astype(F32)
        y_ref[:, g * gw:(g + 1) * gw] = (u * mixed).astype(y_ref.dtype)


def _sgu(z, lng_all, lnb_all, ws_all, bs_t, layer):
    T, dffn = z.shape
    half = dffn // 2
    G, C = GMLP_GROUPS, CHUNK
    vec = pl.BlockSpec((None, 1, half), lambda i: (layer, 0, 0))
    return pl.pallas_call(
        functools.partial(_sgu_kernel, groups=G),
        grid=(T // C,),
        in_specs=[pl.BlockSpec((C, half), lambda i: (i, 0)),
                  pl.BlockSpec((C, half), lambda i: (i, 1)),
                  vec, vec,
                  pl.BlockSpec((None, G, C, C), lambda i: (layer, 0, 0, 0)),
                  pl.BlockSpec((None, C, G), lambda i: (layer, 0, 0))],
        out_specs=pl.BlockSpec((C, half), lambda i: (i, 0)),
        out_shape=jax.ShapeDtypeStruct((T, half), BF16),
        compiler_params=_params(1),
        name="sgu",
    )(z, z, lng_all, lnb_all, ws_all, bs_t)


def _fox_prep_kernel(f_ref, bf_ref, c3_ref, carry_ref, *, group):
    @pl.when(pl.program_id(1) == 0)
    def _():
        carry_ref[...] = jnp.zeros_like(carry_ref)

    log_f = jax.nn.log_sigmoid(f_ref[...] + bf_ref[...])
    tb, heads = log_f.shape
    row = lax.broadcasted_iota(jnp.int32, (tb, tb), 0)
    col = lax.broadcasted_iota(jnp.int32, (tb, tb), 1)
    tri = jnp.where(col <= row, 1.0, 0.0).astype(BF16)

    def split3(x):
        hi = x.astype(BF16)
        r1 = x - hi.astype(F32)
        mid = r1.astype(BF16)
        lo = (r1 - mid.astype(F32)).astype(BF16)
        return hi, mid, lo

    c = carry_ref[...] + sum(jnp.dot(tri, part, preferred_element_type=F32)
                             for part in split3(log_f))
    carry_ref[...] = c[tb - 1:tb, :]
    parts = [part.astype(F32) for part in split3(c * LOG2E)]
    pad = jnp.zeros((tb, LANES - 3 * group), F32)
    for g in range(heads // group):
        c3_ref[g] = jnp.concatenate(
            [part[:, g * group:(g + 1) * group] for part in parts] + [pad], axis=1)


def _fox_prep(f_logit, bf_all, layer, group, batch, seq, tb=512):
    T, H = f_logit.shape
    nb = seq // tb
    return pl.pallas_call(
        functools.partial(_fox_prep_kernel, group=group),
        grid=(batch, nb),
        in_specs=[pl.BlockSpec((tb, H), lambda b, s: (b * nb + s, 0)),
                  pl.BlockSpec((None, 1, H), lambda b, s: (layer, 0, 0))],
        out_specs=pl.BlockSpec((H // group, tb, LANES), lambda b, s: (0, b * nb + s, 0)),
        out_shape=jax.ShapeDtypeStruct((H // group, T, LANES), F32),
        scratch_shapes=[pltpu.VMEM((1, H), F32)],
        compiler_params=_params(2),
        name="fox_prep",
    )(f_logit, bf_all)


def _fox_attn_kernel(qi_tab, ki_tab, qa_ref, ka_ref, va_ref, g_ref, y_ref, m_ref, acc_ref,
                     *, heads, hd):
    p = pl.program_id(1)
    qi = qi_tab[p]
    ki = ki_tab[p]

    @pl.when(ki == 0)
    def _():
        m_ref[...] = jnp.full_like(m_ref, -jnp.inf)
        acc_ref[...] = jnp.zeros_like(acc_ref)

    tq = qa_ref.shape[0]

    def update(h, r0, r1, n_keys, masked):
        wide = slice(2 * h * hd, 2 * (h + 1) * hd)
        s = lax.dot_general(qa_ref[r0:r1, wide], ka_ref[0:n_keys, wide], (((1,), (1,)), ((), ())),
                            preferred_element_type=F32)
        if masked:
            row = lax.broadcasted_iota(jnp.int32, s.shape, 0) + r0
            col = lax.broadcasted_iota(jnp.int32, s.shape, 1)
            s = jnp.where(col <= row, s, -jnp.inf)
        m_prev = m_ref[h, r0:r1]
        m_new = jnp.maximum(m_prev, jnp.max(s, axis=-1, keepdims=True))
        alpha = jnp.exp2(m_prev - m_new)
        pexp = jnp.exp2(s - m_new[:, :1]).astype(BF16)
        pv = jnp.dot(pexp, va_ref[0:n_keys, wide], preferred_element_type=F32)
        acc_ref[h, r0:r1] = jnp.concatenate([alpha, alpha], axis=1) * acc_ref[h, r0:r1] + pv
        m_ref[h, r0:r1] = m_new

    @pl.when(ki < qi)
    def _():
        for h in range(heads):
            update(h, 0, tq, tq, False)

    @pl.when(ki == qi)
    def _():
        for h in range(heads):
            update(h, 0, tq // 2, tq // 2, True)
            update(h, tq // 2, tq, tq, True)
        for h in range(heads):
            sl = slice(h * hd, (h + 1) * hd)
            acc = acc_ref[h]
            o = acc[:, :hd] / acc[:, hd:hd + 1]
            y_ref[:, sl] = (jax.nn.sigmoid(g_ref[:, sl].astype(F32)) * o).astype(y_ref.dtype)


def _fox_attention(qa, ka, va, gate, batch, seq, tq=512):
    T, width = gate.shape
    H = FOX_HEADS
    hd = width // H
    nq = seq // tq
    pairs = [(qi, ki) for qi in range(nq) for ki in range(qi + 1)]
    qi_tab = jnp.asarray([p[0] for p in pairs], jnp.int32)
    ki_tab = jnp.asarray([p[1] for p in pairs], jnp.int32)

    def q_side(cols):
        return pl.BlockSpec((tq, cols), lambda b, p, qt, kt: (b * nq + qt[p], 0))

    def k_side(cols):
        return pl.BlockSpec((tq, cols), lambda b, p, qt, kt: (b * nq + kt[p], 0))

    grid_spec = pltpu.PrefetchScalarGridSpec(
        num_scalar_prefetch=2,
        grid=(batch, len(pairs)),
        in_specs=[q_side(2 * width), k_side(2 * width), k_side(2 * width), q_side(width)],
        out_specs=q_side(width),
        scratch_shapes=[pltpu.VMEM((H, tq, LANES), F32), pltpu.VMEM((H, tq, 2 * hd), F32)],
    )
    return pl.pallas_call(
        functools.partial(_fox_attn_kernel, heads=H, hd=hd),
        grid_spec=grid_spec,
        out_shape=jax.ShapeDtypeStruct((T, width), BF16),
        compiler_params=_params(2),
        name="fox_attention",
    )(qi_tab, ki_tab, qa, ka, va, gate)


def kernel(x, positions, mix_norm_g, ffn_norm_g, ret_w_in, ret_gn_g, ret_w_out, gmlp_w_in, gmlp_ln_g, gmlp_ln_b, gmlp_w_s, gmlp_b_s, gmlp_w_out, fox_w_in, fox_b_f, fox_qn_g, fox_kn_g, fox_w_out, ffn_w_gate, ffn_w_up, ffn_w_down):
    B, S, D = x.shape
    T = B * S
    depth = mix_norm_g.shape[0]
    xf = x.reshape(T, D)
    xn = _cast_stats(xf)

    def rows(a):
        return a.reshape(a.shape[0], 1, a.shape[1])

    mix_g, ffn_g = mix_norm_g[..., None], ffn_norm_g[..., None]
    mix_g_row = rows(mix_norm_g)
    ret_gn, lng, lnb = rows(ret_gn_g), rows(gmlp_ln_g), rows(gmlp_ln_b)
    qn_g, kn_g, fox_bf = rows(fox_qn_g), rows(fox_kn_g), rows(fox_b_f)
    bs_t = jnp.swapaxes(gmlp_b_s, 1, 2)
    fox_w_t = jnp.swapaxes(fox_w_in, 1, 2)

    tm = tn = 1024
    ret_dk = ret_w_in.shape[-1] // (6 * RET_HEADS)
    ret_qk = 2 * RET_HEADS * ret_dk
    inv_freq = (ROPE_BASE ** (-jnp.arange(ret_dk // 2, dtype=F32) / (ret_dk // 2))).reshape(1, -1)
    cos, sin = _rope_tables(positions.reshape(T, 1), inv_freq)
    rope_spec = pl.BlockSpec((tm, ret_dk // 2), lambda j, i: (i, 0))

    hd = fox_qn_g.shape[-1]
    fox_d = FOX_HEADS * hd
    for i in range(depth):
        kind, j = i % N_MIXERS, i // N_MIXERS
        if kind == 0:
            qk = _mm_norm(xn, mix_g, ret_w_in, j, i, ret_qk, epilogue="rope", extra=(cos, sin),
                          extra_specs=(rope_spec, rope_spec), cfg=(ret_dk, ret_qk // 2 // tn),
                          tm=tm, tn=tn, name="mm_ret_qk")
            vg, w_out = _mm_norm(xn, mix_g, ret_w_in, j, i, ret_w_in.shape[-1] - ret_qk,
                                 col0=ret_qk // tn, side=(ret_w_out, j), tm=tm, tn=tn,
                                 name="mm_ret_vg")
            y = _retention(qk, vg, ret_gn, j, B, S)
        elif kind == 1:
            z, w_out = _mm_norm(xn, mix_g, gmlp_w_in, j, i, gmlp_w_in.shape[-1], epilogue="gelu",
                                side=(gmlp_w_out, j), name="mm_gmlp_in")
            y = _sgu(z, lng, lnb, gmlp_w_s, bs_t, j)
        else:
            f_logit = _mm_norm(xn, mix_g_row, fox_w_t, j, i, FOX_HEADS,
                               col0=4 * fox_d // FOX_HEADS, transposed=True, out_dtype=F32,
                               tm=tm, tn=FOX_HEADS, name="mm_fox_forget")
            c3 = _fox_prep(f_logit, fox_bf, j, tn // hd, B, S)
            gain_spec = pl.BlockSpec((None, 1, hd), lambda jj, ii: (j, 0, 0))
            c3_spec = pl.BlockSpec((None, tm, LANES), lambda jj, ii: (jj, ii, 0))
            qa = _mm_norm(xn, mix_g_row, fox_w_t, j, i, fox_d, transposed=True, epilogue="fox_q",
                          extra=(qn_g, c3), extra_specs=(gain_spec, c3_spec),
                          cfg=(hd, hd ** -0.5 * LOG2E), tm=tm, tn=tn, name="mm_fox_q")
            ka = _mm_norm(xn, mix_g_row, fox_w_t, j, i, fox_d, col0=fox_d // tn, transposed=True,
                          epilogue="fox_k", extra=(kn_g, c3), extra_specs=(gain_spec, c3_spec),
                          cfg=(hd, 1.0), tm=tm, tn=tn, name="mm_fox_k")
            va = _mm_norm(xn, mix_g_row, fox_w_t, j, i, fox_d, col0=2 * fox_d // tn,
                          transposed=True, epilogue="fox_v", cfg=(hd,), tm=tm, tn=tn,
                          name="mm_fox_v")
            gate, w_out = _mm_norm(xn, mix_g_row, fox_w_t, j, i, fox_d, col0=3 * fox_d // tn,
                                   transposed=True, side=(fox_w_out, j), tm=tm, tn=tn,
                                   name="mm_fox_gate")
            y = _fox_attention(qa, ka, va, gate, B, S)
        xf, xn = _mm_residual(y, w_out, xf)
        a, w_down = _mm_swiglu(xn, ffn_g, ffn_w_gate, ffn_w_up, i, (ffn_w_down, i))
        xf, xn = _mm_residual(a, w_down, xf)
    return xf.reshape(B, S, D)
```

```python
import functools

import jax
import jax.numpy as jnp
from jax import lax
from jax.experimental import pallas as pl
from jax.experimental.pallas import tpu as pltpu

F32 = jnp.float32
BF16 = jnp.bfloat16

EPS = 1e-6
CHUNK = 128
RET_BLOCK = 256
NORM_ROWS = 32
ROPE_BASE = 10000.0
RET_HEADS = 8
GMLP_GROUPS = 8
FOX_HEADS = 16
N_MIXERS = 3
LOG2E = 1.4426950408889634

LANES = 128
VMEM_LIMIT = 56 * 1024 * 1024


def _params(n_axes):
    return pltpu.CompilerParams(
        dimension_semantics=("arbitrary",) * n_axes, vmem_limit_bytes=VMEM_LIMIT)


def _gelu_exact(x):
    return 0.5 * x * (1.0 + lax.erf(x * (2.0 ** -0.5)))


def _prepare_norm_matmul(x_ref, g_ref, ssq_ref, w_refs, wb_refs, side_refs):
    @pl.when(pl.program_id(1) == 0)
    def _():
        for w_ref, wb_ref in zip(w_refs, wb_refs):
            wb_ref[...] = (w_ref[...] * g_ref[...]).astype(BF16)

    if side_refs:
        src_ref, dst_ref = side_refs
        dst_ref[...] = src_ref[...].astype(BF16)
    ssq = jnp.sum(ssq_ref[...], axis=0)[:, :1]
    return lax.rsqrt(ssq * (1.0 / x_ref.shape[1]) + EPS)


def _side_cast_specs(side, n_j, n_i):
    w_all, layer = side
    K, N = w_all.shape[1:]
    n_steps = n_j * n_i
    rc = K // n_steps
    assert rc * n_steps == K and rc % 16 == 0, (K, n_steps)
    return (pl.BlockSpec((None, rc, N), lambda j, i: (layer, j * n_i + i, 0)),
            pl.BlockSpec((rc, N), lambda j, i: (j * n_i + i, 0)),
            jax.ShapeDtypeStruct((K, N), BF16))


def _mm_norm_kernel(*refs, epilogue, transposed, cfg, has_side):
    x_ref, g_ref, w_ref, ssq_ref = refs[:4]
    wb_ref = refs[-1]
    if has_side:
        extra, side_src = refs[4:-4], refs[-4]
        o_ref, side_dst = refs[-3], refs[-2]
        side_refs = (side_src, side_dst)
    else:
        extra, o_ref, side_refs = refs[4:-2], refs[-2], ()
    r = _prepare_norm_matmul(x_ref, g_ref, ssq_ref, (w_ref,), (wb_ref,), side_refs)
    if transposed:
        acc = lax.dot_general(x_ref[...], wb_ref[...], (((1,), (1,)), ((), ())),
                              preferred_element_type=F32) * r
    else:
        acc = jnp.dot(x_ref[...], wb_ref[...], preferred_element_type=F32) * r
    if epilogue == "gelu":
        o_ref[...] = _gelu_exact(acc).astype(o_ref.dtype)
    elif epilogue in ("rope_q", "rope_kt"):
        cos_ref, sin_ref = extra
        dk = cfg[0]
        half = dk // 2
        scale = 1.0 if epilogue == "rope_q" else dk ** -0.5
        cos = cos_ref[...] * scale
        sin = sin_ref[...] * scale
        for h in range(acc.shape[1] // dk):
            t1 = acc[:, h * dk:h * dk + half]
            t2 = acc[:, h * dk + half:(h + 1) * dk]
            r1, r2 = t1 * cos - t2 * sin, t2 * cos + t1 * sin
            if epilogue == "rope_q":
                o_ref[:, h * dk:h * dk + half] = r1.astype(o_ref.dtype)
                o_ref[:, h * dk + half:(h + 1) * dk] = r2.astype(o_ref.dtype)
            else:
                o_ref[h * dk:h * dk + half, :] = r1.T.astype(o_ref.dtype)
                o_ref[h * dk + half:(h + 1) * dk, :] = r2.T.astype(o_ref.dtype)
    elif epilogue in ("fox_q", "fox_k", "fox_v"):
        hd = cfg[0]
        heads = acc.shape[1] // hd
        lane = lax.broadcasted_iota(jnp.int32, (acc.shape[0], hd), 1)
        if epilogue == "fox_v":
            ones_col = jnp.where(lane == 0, 1.0, 0.0).astype(o_ref.dtype)
        else:
            gain_ref, c3_ref = extra
            gain = gain_ref[...] * cfg[1]
        for h in range(heads):
            t = acc[:, h * hd:(h + 1) * hd]
            if epilogue == "fox_v":
                data, aug = t, ones_col
            else:
                data = t * lax.rsqrt(jnp.mean(t * t, axis=-1, keepdims=True) + EPS) * gain
                hi, mid, lo = (c3_ref[:, p * heads + h:p * heads + h + 1] for p in range(3))
                if epilogue == "fox_q":
                    aug = jnp.where(lane == 0, hi, jnp.where(lane == 1, mid, jnp.where(
                        lane == 2, lo, jnp.where(lane < 6, 1.0, 0.0))))
                else:
                    aug = jnp.where(lane < 3, 1.0, jnp.where(lane == 3, -hi, jnp.where(
                        lane == 4, -mid, jnp.where(lane == 5, -lo, 0.0))))
            o_ref[:, 2 * h * hd:(2 * h + 1) * hd] = data.astype(o_ref.dtype)
            o_ref[:, (2 * h + 1) * hd:(2 * h + 2) * hd] = aug.astype(o_ref.dtype)
    else:
        o_ref[...] = acc.astype(o_ref.dtype)


def _ssq_spec(ssq, tm):
    return pl.BlockSpec((ssq.shape[0], tm, LANES), lambda j, i: (0, i, 0))


def _mm_norm(x, gain, w_all, layer, g_layer, n_out, *, col0=0, transposed=False, epilogue=None,
             extra=(), extra_specs=(), cfg=None, side=None, out_dtype=BF16, tm=1024, tn=1024,
             name="mm_norm"):
    xb, ssq = x
    T, K = xb.shape
    n_j, n_i = n_out // tn, T // tm
    if transposed:
        g_spec = pl.BlockSpec((None, 1, K), lambda j, i: (g_layer, 0, 0))
        w_spec = pl.BlockSpec((None, tn, K), lambda j, i: (layer, col0 + j, 0))
        wb_shape = (tn, K)
    else:
        g_spec = pl.BlockSpec((None, K, 1), lambda j, i: (g_layer, 0, 0))
        w_spec = pl.BlockSpec((None, K, tn), lambda j, i: (layer, 0, col0 + j))
        wb_shape = (K, tn)
    in_specs = [pl.BlockSpec((tm, K), lambda j, i: (i, 0)), g_spec, w_spec, _ssq_spec(ssq, tm),
                *extra_specs]
    operands = [xb, gain, w_all, ssq, *extra]
    widen = 2 if epilogue in ("fox_q", "fox_k", "fox_v") else 1
    if epilogue == "rope_kt":
        out_specs = [pl.BlockSpec((widen * tn, tm), lambda j, i: (j, i))]
        out_shape = [jax.ShapeDtypeStruct((widen * n_out, T), out_dtype)]
    else:
        out_specs = [pl.BlockSpec((tm, widen * tn), lambda j, i: (i, j))]
        out_shape = [jax.ShapeDtypeStruct((T, widen * n_out), out_dtype)]
    if side is not None:
        src_spec, dst_spec, dst_shape = _side_cast_specs(side, n_j, n_i)
        in_specs.append(src_spec)
        operands.append(side[0])
        out_specs.append(dst_spec)
        out_shape.append(dst_shape)
    out = pl.pallas_call(
        functools.partial(_mm_norm_kernel, epilogue=epilogue, transposed=transposed, cfg=cfg,
                          has_side=side is not None),
        grid=(n_j, n_i),
        in_specs=in_specs,
        out_specs=out_specs,
        out_shape=out_shape,
        scratch_shapes=[pltpu.VMEM(wb_shape, BF16)],
        compiler_params=_params(2),
        name=name,
    )(*operands)
    return out if side is not None else out[0]


def _mm_swiglu_kernel(x_ref, g_ref, wg_ref, wu_ref, ssq_ref, side_src, o_ref, side_dst,
                      wgb_ref, wub_ref):
    r = _prepare_norm_matmul(x_ref, g_ref, ssq_ref, (wg_ref, wu_ref), (wgb_ref, wub_ref),
                             (side_src, side_dst))
    x = x_ref[...]
    gate = jnp.dot(x, wgb_ref[...], preferred_element_type=F32) * r
    up = jnp.dot(x, wub_ref[...], preferred_element_type=F32) * r
    o_ref[...] = (jax.nn.silu(gate) * up).astype(o_ref.dtype)


def _mm_swiglu(x, g_col, wg_all, wu_all, layer, side, *, tm=1024, tn=512):
    xb, ssq = x
    T, K = xb.shape
    F = wg_all.shape[-1]
    n_j, n_i = F // tn, T // tm
    wspec = pl.BlockSpec((None, K, tn), lambda j, i: (layer, 0, j))
    src_spec, dst_spec, dst_shape = _side_cast_specs(side, n_j, n_i)
    return pl.pallas_call(
        _mm_swiglu_kernel,
        grid=(n_j, n_i),
        in_specs=[pl.BlockSpec((tm, K), lambda j, i: (i, 0)),
                  pl.BlockSpec((None, K, 1), lambda j, i: (layer, 0, 0)),
                  wspec, wspec, _ssq_spec(ssq, tm), src_spec],
        out_specs=[pl.BlockSpec((tm, tn), lambda j, i: (i, j)), dst_spec],
        out_shape=[jax.ShapeDtypeStruct((T, F), BF16), dst_shape],
        scratch_shapes=[pltpu.VMEM((K, tn), BF16), pltpu.VMEM((K, tn), BF16)],
        compiler_params=_params(2),
        name="mm_swiglu",
    )(xb, g_col, wg_all, wu_all, ssq, side[0])


def _row_ssq(x):
    return jnp.broadcast_to(jnp.sum(x * x, axis=-1, keepdims=True), (x.shape[0], LANES))


def _cast_stats_kernel(x_ref, xb_ref, ssq_ref):
    x = x_ref[...]
    xb_ref[...] = x.astype(BF16)
    ssq_ref[...] = _row_ssq(x)


def _cast_stats(x, tm=512):
    T, D = x.shape
    return pl.pallas_call(
        _cast_stats_kernel,
        grid=(T // tm,),
        in_specs=[pl.BlockSpec((tm, D), lambda i: (i, 0))],
        out_specs=[pl.BlockSpec((tm, D), lambda i: (i, 0)),
                   pl.BlockSpec((None, tm, LANES), lambda i: (0, i, 0))],
        out_shape=[jax.ShapeDtypeStruct((T, D), BF16), jax.ShapeDtypeStruct((1, T, LANES), F32)],
        compiler_params=_params(1),
        name="cast_stats",
    )(x)


def _mm_residual_kernel(a_ref, w_ref, r_ref, o_ref, *norm_refs):
    x_new = r_ref[...] + jnp.dot(a_ref[...], w_ref[...], preferred_element_type=F32)
    o_ref[...] = x_new
    if norm_refs:
        ob_ref, ssq_ref = norm_refs
        ob_ref[...] = x_new.astype(BF16)
        ssq_ref[...] = _row_ssq(x_new)


def _residual_tiles(K, N):
    return 512, (N if K <= 2048 else N // 2)


def _mm_residual(a, wb, res, last=False):
    T, K = a.shape
    N = wb.shape[-1]
    tm, tn = _residual_tiles(K, N)
    tile = pl.BlockSpec((tm, tn), lambda j, i: (i, j))
    out_specs = [tile, tile, pl.BlockSpec((None, tm, LANES), lambda j, i: (j, i, 0))]
    out_shape = [jax.ShapeDtypeStruct((T, N), F32), jax.ShapeDtypeStruct((T, N), BF16),
                 jax.ShapeDtypeStruct((N // tn, T, LANES), F32)]
    n_out = 1 if last else 3
    out = pl.pallas_call(
        _mm_residual_kernel,
        grid=(N // tn, T // tm),
        in_specs=[pl.BlockSpec((tm, K), lambda j, i: (i, 0)),
                  pl.BlockSpec((K, tn), lambda j, i: (0, j)),
                  tile],
        out_specs=out_specs[:n_out],
        out_shape=out_shape[:n_out],
        compiler_params=_params(2),
        name="mm_residual",
    )(a, wb, res)
    return out[0], (None if last else (out[1], out[2]))


def _rope_kernel(pos_ref, invf_ref, cos_ref, sin_ref):
    ang = pos_ref[...].astype(F32) * invf_ref[...]
    cos_ref[...] = jnp.cos(ang)
    sin_ref[...] = jnp.sin(ang)


def _rope_tables(pos_col, inv_freq, tm=512):
    T = pos_col.shape[0]
    half = inv_freq.shape[-1]
    out = jax.ShapeDtypeStruct((T, half), F32)
    return pl.pallas_call(
        _rope_kernel,
        grid=(T // tm,),
        in_specs=[pl.BlockSpec((tm, 1), lambda i: (i, 0)),
                  pl.BlockSpec((1, half), lambda i: (0, 0))],
        out_specs=[pl.BlockSpec((tm, half), lambda i: (i, 0))] * 2,
        out_shape=[out, out],
        compiler_params=_params(1),
        name="rope_tables",
    )(pos_col, inv_freq)


def _retention_kernel(q_ref, kt_ref, vg_ref, din_ref, qd_ref, kd_ref, cd_ref, gn_ref,
                      y_ref, state_ref, *, heads, dk, dv):
    @pl.when(pl.program_id(1) == 0)
    def _():
        state_ref[...] = jnp.zeros_like(state_ref)

    g_off = heads * dv
    rows = q_ref.shape[0]
    for h in range(heads):
        qb = q_ref[:, h * dk:(h + 1) * dk]
        kt = kt_ref[h * dk:(h + 1) * dk, :]
        v = vg_ref[:, h * dv:(h + 1) * dv]
        scores = jnp.dot(qb, kt, preferred_element_type=F32) * din_ref[h]
        state = state_ref[h]
        q_dec = (qb.astype(F32) * qd_ref[h]).astype(BF16)
        o = (jnp.dot(scores.astype(BF16), v, preferred_element_type=F32)
             + jnp.dot(q_dec, state.astype(BF16), preferred_element_type=F32))
        kt_dec = (kt.astype(F32) * kd_ref[h]).astype(BF16)
        state_ref[h] = state * cd_ref[h] + jnp.dot(kt_dec, v, preferred_element_type=F32)
        gain = gn_ref[:, h * dv:(h + 1) * dv]
        for r0 in range(0, rows, NORM_ROWS):
            strip = o[r0:r0 + NORM_ROWS]
            sc = strip - jnp.mean(strip, axis=-1, keepdims=True)
            var = jnp.mean(sc * sc, axis=-1, keepdims=True)
            g = vg_ref[r0:r0 + NORM_ROWS, g_off + h * dv:g_off + (h + 1) * dv].astype(F32)
            y_ref[r0:r0 + NORM_ROWS, h * dv:(h + 1) * dv] = (
                jax.nn.silu(g) * (sc * lax.rsqrt(var + EPS) * gain)).astype(y_ref.dtype)


def _retention(q, kt, vg, gn_all, layer, batch, seq):
    T = q.shape[0]
    H, C = RET_HEADS, RET_BLOCK
    dk = q.shape[1] // H
    dv = vg.shape[1] // (2 * H)
    nc = seq // C
    log_gamma = jnp.log1p(-jnp.exp2(-5.0 - jnp.arange(H, dtype=F32)))
    idx = jnp.arange(C, dtype=F32)
    dist = idx[:, None] - idx[None, :]
    decay_in = jnp.where(dist >= 0,
                         jnp.exp(jnp.maximum(dist, 0.0)[None] * log_gamma[:, None, None]), 0.0)
    q_decay = jnp.exp((idx + 1.0)[None, :] * log_gamma[:, None])[..., None]
    k_decay = jnp.exp((C - 1.0 - idx)[None, :] * log_gamma[:, None])[:, None, :]
    chunk_decay = jnp.exp(C * log_gamma)[:, None, None]

    def whole(shape):
        return pl.BlockSpec(shape, lambda b, c: (0,) * len(shape))

    return pl.pallas_call(
        functools.partial(_retention_kernel, heads=H, dk=dk, dv=dv),
        grid=(batch, nc),
        in_specs=[pl.BlockSpec((C, H * dk), lambda b, c: (b * nc + c, 0)),
                  pl.BlockSpec((H * dk, C), lambda b, c: (0, b * nc + c)),
                  pl.BlockSpec((C, 2 * H * dv), lambda b, c: (b * nc + c, 0)),
                  whole((H, C, C)), whole((H, C, 1)), whole((H, 1, C)), whole((H, 1, 1)),
                  pl.BlockSpec((None, 1, H * dv), lambda b, c: (layer, 0, 0))],
        out_specs=pl.BlockSpec((C, H * dv), lambda b, c: (b * nc + c, 0)),
        out_shape=jax.ShapeDtypeStruct((T, H * dv), BF16),
        scratch_shapes=[pltpu.VMEM((H, dk, dv), F32)],
        compiler_params=_params(2),
        name="retention",
    )(q, kt, vg, decay_in, q_decay, k_decay, chunk_decay, gn_all)


def _sgu_kernel(u_ref, v_ref, lng_ref, lnb_ref, ws_ref, bs_ref, y_ref, *, groups):
    C = ws_ref.shape[-1]
    gw = v_ref.shape[1] // groups
    row = lax.broadcasted_iota(jnp.int32, (C, C), 0)
    col = lax.broadcasted_iota(jnp.int32, (C, C), 1)
    w_causal = [jnp.where(col <= row, ws_ref[g], 0.0).astype(BF16) for g in range(groups)]
    for c0 in range(0, v_ref.shape[0], C):
        rows = slice(c0, c0 + C)
        v = v_ref[rows, :].astype(F32)
        vc = v - jnp.mean(v, axis=-1, keepdims=True)
        var = jnp.mean(vc * vc, axis=-1, keepdims=True)
        vn = (vc * lax.rsqrt(var + EPS) * lng_ref[...] + lnb_ref[...]).astype(BF16)
        for g in range(groups):
            cols = slice(g * gw, (g + 1) * gw)
            mixed = jnp.dot(w_causal[g], vn[:, cols], preferred_element_type=F32)
            mixed = mixed + bs_ref[:, g:g + 1]
            y_ref[rows, cols] = (u_ref[rows, cols].astype(F32) * mixed).astype(y_ref.dtype)


def _sgu(z, lng_all, lnb_all, ws_all, bs_t, layer):
    T, dffn = z.shape
    half = dffn // 2
    G, C = GMLP_GROUPS, CHUNK
    R = 2 * C
    vec = pl.BlockSpec((None, 1, half), lambda i: (layer, 0, 0))
    return pl.pallas_call(
        functools.partial(_sgu_kernel, groups=G),
        grid=(T // R,),
        in_specs=[pl.BlockSpec((R, half), lambda i: (i, 0)),
                  pl.BlockSpec((R, half), lambda i: (i, 1)),
                  vec, vec,
                  pl.BlockSpec((None, G, C, C), lambda i: (layer, 0, 0, 0)),
                  pl.BlockSpec((None, C, G), lambda i: (layer, 0, 0))],
        out_specs=pl.BlockSpec((R, half), lambda i: (i, 0)),
        out_shape=jax.ShapeDtypeStruct((T, half), BF16),
        compiler_params=_params(1),
        name="sgu",
    )(z, z, lng_all, lnb_all, ws_all, bs_t)


def _fox_prep_kernel(f_ref, bf_ref, c3_ref, carry_ref, *, group):
    @pl.when(pl.program_id(1) == 0)
    def _():
        carry_ref[...] = jnp.zeros_like(carry_ref)

    log_f = jax.nn.log_sigmoid(f_ref[...] + bf_ref[...])
    tb, heads = log_f.shape
    row = lax.broadcasted_iota(jnp.int32, (tb, tb), 0)
    col = lax.broadcasted_iota(jnp.int32, (tb, tb), 1)
    tri = jnp.where(col <= row, 1.0, 0.0).astype(BF16)

    def split3(x):
        hi = x.astype(BF16)
        r1 = x - hi.astype(F32)
        mid = r1.astype(BF16)
        lo = (r1 - mid.astype(F32)).astype(BF16)
        return hi, mid, lo

    c = carry_ref[...] + sum(jnp.dot(tri, part, preferred_element_type=F32)
                             for part in split3(log_f))
    carry_ref[...] = c[tb - 1:tb, :]
    parts = [part.astype(F32) for part in split3(c * LOG2E)]
    pad = jnp.zeros((tb, LANES - 3 * group), F32)
    for g in range(heads // group):
        c3_ref[g] = jnp.concatenate(
            [part[:, g * group:(g + 1) * group] for part in parts] + [pad], axis=1)


def _fox_prep(f_logit, bf_all, layer, group, batch, seq, tb=512):
    T, H = f_logit.shape
    nb = seq // tb
    return pl.pallas_call(
        functools.partial(_fox_prep_kernel, group=group),
        grid=(batch, nb),
        in_specs=[pl.BlockSpec((tb, H), lambda b, s: (b * nb + s, 0)),
                  pl.BlockSpec((None, 1, H), lambda b, s: (layer, 0, 0))],
        out_specs=pl.BlockSpec((H // group, tb, LANES), lambda b, s: (0, b * nb + s, 0)),
        out_shape=jax.ShapeDtypeStruct((H // group, T, LANES), F32),
        scratch_shapes=[pltpu.VMEM((1, H), F32)],
        compiler_params=_params(2),
        name="fox_prep",
    )(f_logit, bf_all)


def _fox_attn_kernel(qi_tab, ki_tab, qa_ref, ka_ref, va_ref, g_ref, y_ref, m_ref, acc_ref,
                     *, heads, hd):
    p = pl.program_id(1)
    qi = qi_tab[p]
    ki = ki_tab[p]

    @pl.when(ki == 0)
    def _():
        m_ref[...] = jnp.full_like(m_ref, -jnp.inf)
        acc_ref[...] = jnp.zeros_like(acc_ref)

    tq = qa_ref.shape[0]

    def update(h, r0, r1, n_keys, masked):
        wide = slice(2 * h * hd, 2 * (h + 1) * hd)
        s = lax.dot_general(qa_ref[r0:r1, wide], ka_ref[0:n_keys, wide], (((1,), (1,)), ((), ())),
                            preferred_element_type=F32)
        if masked:
            row = lax.broadcasted_iota(jnp.int32, s.shape, 0) + r0
            col = lax.broadcasted_iota(jnp.int32, s.shape, 1)
            s = jnp.where(col <= row, s, -jnp.inf)
        m_prev = m_ref[h, r0:r1]
        m_new = jnp.maximum(m_prev, jnp.max(s, axis=-1, keepdims=True))
        alpha = jnp.exp2(m_prev - m_new)
        pexp = jnp.exp2(s - m_new[:, :1]).astype(BF16)
        pv = jnp.dot(pexp, va_ref[0:n_keys, wide], preferred_element_type=F32)
        acc_ref[h, r0:r1] = jnp.concatenate([alpha, alpha], axis=1) * acc_ref[h, r0:r1] + pv
        m_ref[h, r0:r1] = m_new

    @pl.when(ki < qi)
    def _():
        for h in range(heads):
            update(h, 0, tq, tq, False)

    @pl.when(ki == qi)
    def _():
        for h in range(heads):
            update(h, 0, tq // 2, tq // 2, True)
            update(h, tq // 2, tq, tq, True)
        for h in range(heads):
            sl = slice(h * hd, (h + 1) * hd)
            acc = acc_ref[h]
            o = acc[:, :hd] / acc[:, hd:hd + 1]
            y_ref[:, sl] = (jax.nn.sigmoid(g_ref[:, sl].astype(F32)) * o).astype(y_ref.dtype)


def _fox_attention(qa, ka, va, gate, batch, seq, tq=512):
    T, width = gate.shape
    H = FOX_HEADS
    hd = width // H
    nq = seq // tq
    pairs = [(qi, ki) for qi in range(nq) for ki in range(qi + 1)]
    qi_tab = jnp.asarray([p[0] for p in pairs], jnp.int32)
    ki_tab = jnp.asarray([p[1] for p in pairs], jnp.int32)

    def q_side(cols):
        return pl.BlockSpec((tq, cols), lambda b, p, qt, kt: (b * nq + qt[p], 0))

    def k_side(cols):
        return pl.BlockSpec((tq, cols), lambda b, p, qt, kt: (b * nq + kt[p], 0))

    grid_spec = pltpu.PrefetchScalarGridSpec(
        num_scalar_prefetch=2,
        grid=(batch, len(pairs)),
        in_specs=[q_side(2 * width), k_side(2 * width), k_side(2 * width), q_side(width)],
        out_specs=q_side(width),
        scratch_shapes=[pltpu.VMEM((H, tq, LANES), F32), pltpu.VMEM((H, tq, 2 * hd), F32)],
    )
    return pl.pallas_call(
        functools.partial(_fox_attn_kernel, heads=H, hd=hd),
        grid_spec=grid_spec,
        out_shape=jax.ShapeDtypeStruct((T, width), BF16),
        compiler_params=_params(2),
        name="fox_attention",
    )(qi_tab, ki_tab, qa, ka, va, gate)


def kernel(x, positions, mix_norm_g, ffn_norm_g, ret_w_in, ret_gn_g, ret_w_out, gmlp_w_in, gmlp_ln_g, gmlp_ln_b, gmlp_w_s, gmlp_b_s, gmlp_w_out, fox_w_in, fox_b_f, fox_qn_g, fox_kn_g, fox_w_out, ffn_w_gate, ffn_w_up, ffn_w_down):
    B, S, D = x.shape
    T = B * S
    depth = mix_norm_g.shape[0]
    xf = x.reshape(T, D)
    xn = _cast_stats(xf)

    def rows(a):
        return a.reshape(a.shape[0], 1, a.shape[1])

    mix_g, ffn_g = mix_norm_g[..., None], ffn_norm_g[..., None]
    mix_g_row = rows(mix_norm_g)
    ret_gn, lng, lnb = rows(ret_gn_g), rows(gmlp_ln_g), rows(gmlp_ln_b)
    qn_g, kn_g, fox_bf = rows(fox_qn_g), rows(fox_kn_g), rows(fox_b_f)
    bs_t = jnp.swapaxes(gmlp_b_s, 1, 2)
    fox_w_t = jnp.swapaxes(fox_w_in, 1, 2)

    tm = tn = 1024
    ret_dk = ret_w_in.shape[-1] // (6 * RET_HEADS)
    ret_qk = 2 * RET_HEADS * ret_dk
    inv_freq = (ROPE_BASE ** (-jnp.arange(ret_dk // 2, dtype=F32) / (ret_dk // 2))).reshape(1, -1)
    cos, sin = _rope_tables(positions.reshape(T, 1), inv_freq)
    rope_spec = pl.BlockSpec((tm, ret_dk // 2), lambda j, i: (i, 0))

    hd = fox_qn_g.shape[-1]
    fox_d = FOX_HEADS * hd
    for i in range(depth):
        kind, j = i % N_MIXERS, i // N_MIXERS
        if kind == 0:
            rope = dict(extra=(cos, sin), extra_specs=(rope_spec, rope_spec), cfg=(ret_dk,),
                        tm=tm, tn=tn)
            q = _mm_norm(xn, mix_g, ret_w_in, j, i, ret_qk // 2, epilogue="rope_q",
                         name="mm_ret_q", **rope)
            kt = _mm_norm(xn, mix_g, ret_w_in, j, i, ret_qk // 2, col0=ret_qk // 2 // tn,
                          epilogue="rope_kt", name="mm_ret_k", **rope)
            vg, w_out = _mm_norm(xn, mix_g, ret_w_in, j, i, ret_w_in.shape[-1] - ret_qk,
                                 col0=ret_qk // tn, side=(ret_w_out, j), tm=tm, tn=tn,
                                 name="mm_ret_vg")
            y = _retention(q, kt, vg, ret_gn, j, B, S)
        elif kind == 1:
            z, w_out = _mm_norm(xn, mix_g, gmlp_w_in, j, i, gmlp_w_in.shape[-1], epilogue="gelu",
                                side=(gmlp_w_out, j), name="mm_gmlp_in")
            y = _sgu(z, lng, lnb, gmlp_w_s, bs_t, j)
        else:
            f_logit = _mm_norm(xn, mix_g_row, fox_w_t, j, i, FOX_HEADS,
                               col0=4 * fox_d // FOX_HEADS, transposed=True, out_dtype=F32,
                               tm=tm, tn=FOX_HEADS, name="mm_fox_forget")
            c3 = _fox_prep(f_logit, fox_bf, j, tn // hd, B, S)
            gain_spec = pl.BlockSpec((None, 1, hd), lambda jj, ii: (j, 0, 0))
            c3_spec = pl.BlockSpec((None, tm, LANES), lambda jj, ii: (jj, ii, 0))
            qa = _mm_norm(xn, mix_g_row, fox_w_t, j, i, fox_d, transposed=True, epilogue="fox_q",
                          extra=(qn_g, c3), extra_specs=(gain_spec, c3_spec),
                          cfg=(hd, hd ** -0.5 * LOG2E), tm=tm, tn=tn, name="mm_fox_q")
            ka = _mm_norm(xn, mix_g_row, fox_w_t, j, i, fox_d, col0=fox_d // tn, transposed=True,
                          epilogue="fox_k", extra=(kn_g, c3), extra_specs=(gain_spec, c3_spec),
                          cfg=(hd, 1.0), tm=tm, tn=tn, name="mm_fox_k")
            va = _mm_norm(xn, mix_g_row, fox_w_t, j, i, fox_d, col0=2 * fox_d // tn,
                          transposed=True, epilogue="fox_v", cfg=(hd,), tm=tm, tn=tn,
                          name="mm_fox_v")
            gate, w_out = _mm_norm(xn, mix_g_row, fox_w_t, j, i, fox_d, col0=3 * fox_d // tn,
                                   transposed=True, side=(fox_w_out, j), tm=tm, tn=tn,
                                   name="mm_fox_gate")
            y = _fox_attention(qa, ka, va, gate, B, S)
        xf, xn = _mm_residual(y, w_out, xf)
        a, w_down = _mm_swiglu(xn, ffn_g, ffn_w_gate, ffn_w_up, i, (ffn_w_down, i))
        xf, xn = _mm_residual(a, w_down, xf, last=i == depth - 1)
    return xf.reshape(B, S, D)
```

```python
import functools

import jax
import jax.numpy as jnp
from jax import lax
from jax.experimental import pallas as pl
from jax.experimental.pallas import tpu as pltpu

F32 = jnp.float32
BF16 = jnp.bfloat16

EPS = 1e-6
CHUNK = 128
RET_BLOCK = 256
NORM_ROWS = 32
SWIGLU_ROWS = 1024
ROPE_BASE = 10000.0
RET_HEADS = 8
GMLP_GROUPS = 8
FOX_HEADS = 16
N_MIXERS = 3
LOG2E = 1.4426950408889634

LANES = 128
VMEM_LIMIT = 56 * 1024 * 1024


def _params(n_axes):
    return pltpu.CompilerParams(
        dimension_semantics=("arbitrary",) * n_axes, vmem_limit_bytes=VMEM_LIMIT)


def _gelu_exact(x):
    return 0.5 * x * (1.0 + lax.erf(x * (2.0 ** -0.5)))


def _row_scale(x_ref, ssq_ref):
    ssq = jnp.sum(ssq_ref[...], axis=0)[:, :1]
    return lax.rsqrt(ssq * (1.0 / x_ref.shape[1]) + EPS)


def _side_casts(src_refs, dst_refs):
    for src_ref, dst_ref in zip(src_refs, dst_refs):
        dst_ref[...] = src_ref[...].astype(BF16)


def _side_cast_specs(side, n_j, n_i):
    w_all, layer = side
    K, N = w_all.shape[1:]
    n_steps = n_j * n_i
    rc = next(r for r in range(16, K + 1, 16) if K % r == 0 and K // r <= n_steps)
    last = K // rc - 1
    return (pl.BlockSpec((None, rc, N), lambda j, i: (layer, jnp.minimum(j * n_i + i, last), 0)),
            pl.BlockSpec((rc, N), lambda j, i: (jnp.minimum(j * n_i + i, last), 0)),
            jax.ShapeDtypeStruct((K, N), BF16))


def _mm_norm_kernel(*refs, epilogue, transposed, cfg, n_side):
    x_ref, g_ref, w_ref, ssq_ref = refs[:4]
    wb_ref = refs[-1]
    o_ref = refs[-2 - n_side]
    extra = refs[4:len(refs) - 2 - 2 * n_side]
    _side_casts(refs[len(refs) - 2 - 2 * n_side:len(refs) - 2 - n_side], refs[len(refs) - 1 - n_side:-1])

    @pl.when(pl.program_id(1) == 0)
    def _():
        wb_ref[...] = (w_ref[...] * g_ref[...]).astype(BF16)

    r = _row_scale(x_ref, ssq_ref)
    if transposed:
        acc = lax.dot_general(x_ref[...], wb_ref[...], (((1,), (1,)), ((), ())),
                              preferred_element_type=F32) * r
    else:
        acc = jnp.dot(x_ref[...], wb_ref[...], preferred_element_type=F32) * r
    if epilogue == "gelu":
        o_ref[...] = _gelu_exact(acc).astype(o_ref.dtype)
    elif epilogue in ("rope_q", "rope_kt"):
        cos_ref, sin_ref = extra
        dk = cfg[0]
        half = dk // 2
        scale = 1.0 if epilogue == "rope_q" else dk ** -0.5
        cos = cos_ref[...] * scale
        sin = sin_ref[...] * scale
        for h in range(acc.shape[1] // dk):
            t1 = acc[:, h * dk:h * dk + half]
            t2 = acc[:, h * dk + half:(h + 1) * dk]
            r1, r2 = t1 * cos - t2 * sin, t2 * cos + t1 * sin
            if epilogue == "rope_q":
                o_ref[:, h * dk:h * dk + half] = r1.astype(o_ref.dtype)
                o_ref[:, h * dk + half:(h + 1) * dk] = r2.astype(o_ref.dtype)
            else:
                o_ref[h * dk:h * dk + half, :] = r1.T.astype(o_ref.dtype)
                o_ref[h * dk + half:(h + 1) * dk, :] = r2.T.astype(o_ref.dtype)
    elif epilogue in ("fox_q", "fox_k", "fox_v"):
        hd = cfg[0]
        heads = acc.shape[1] // hd
        lane = lax.broadcasted_iota(jnp.int32, (acc.shape[0], hd), 1)
        if epilogue == "fox_v":
            ones_col = jnp.where(lane == 0, 1.0, 0.0).astype(o_ref.dtype)
        else:
            gain_ref, c3_ref = extra
            gain = gain_ref[...] * cfg[1]
        for h in range(heads):
            t = acc[:, h * hd:(h + 1) * hd]
            if epilogue == "fox_v":
                data, aug = t, ones_col
            else:
                data = t * lax.rsqrt(jnp.mean(t * t, axis=-1, keepdims=True) + EPS) * gain
                hi, mid, lo = (c3_ref[:, p * heads + h:p * heads + h + 1] for p in range(3))
                if epilogue == "fox_q":
                    aug = jnp.where(lane == 0, hi, jnp.where(lane == 1, mid, jnp.where(
                        lane == 2, lo, jnp.where(lane < 6, 1.0, 0.0))))
                else:
                    aug = jnp.where(lane < 3, 1.0, jnp.where(lane == 3, -hi, jnp.where(
                        lane == 4, -mid, jnp.where(lane == 5, -lo, 0.0))))
            o_ref[:, 2 * h * hd:(2 * h + 1) * hd] = data.astype(o_ref.dtype)
            o_ref[:, (2 * h + 1) * hd:(2 * h + 2) * hd] = aug.astype(o_ref.dtype)
    else:
        o_ref[...] = acc.astype(o_ref.dtype)


def _ssq_spec(ssq, tm):
    return pl.BlockSpec((ssq.shape[0], tm, LANES), lambda j, i: (0, i, 0))


def _mm_norm(x, gain, w_all, layer, g_layer, n_out, *, col0=0, transposed=False, epilogue=None,
             extra=(), extra_specs=(), cfg=None, sides=(), out_dtype=BF16, tm=1024, tn=1024,
             name="mm_norm"):
    xb, ssq = x
    T, K = xb.shape
    n_j, n_i = n_out // tn, T // tm
    if transposed:
        g_spec = pl.BlockSpec((None, 1, K), lambda j, i: (g_layer, 0, 0))
        w_spec = pl.BlockSpec((None, tn, K), lambda j, i: (layer, col0 + j, 0))
        wb_shape = (tn, K)
    else:
        g_spec = pl.BlockSpec((None, K, 1), lambda j, i: (g_layer, 0, 0))
        w_spec = pl.BlockSpec((None, K, tn), lambda j, i: (layer, 0, col0 + j))
        wb_shape = (K, tn)
    in_specs = [pl.BlockSpec((tm, K), lambda j, i: (i, 0)), g_spec, w_spec, _ssq_spec(ssq, tm),
                *extra_specs]
    operands = [xb, gain, w_all, ssq, *extra]
    widen = 2 if epilogue in ("fox_q", "fox_k", "fox_v") else 1
    if epilogue == "rope_kt":
        out_specs = [pl.BlockSpec((widen * tn, tm), lambda j, i: (j, i))]
        out_shape = [jax.ShapeDtypeStruct((widen * n_out, T), out_dtype)]
    else:
        out_specs = [pl.BlockSpec((tm, widen * tn), lambda j, i: (i, j))]
        out_shape = [jax.ShapeDtypeStruct((T, widen * n_out), out_dtype)]
    for side in sides:
        src_spec, dst_spec, dst_shape = _side_cast_specs(side, n_j, n_i)
        in_specs.append(src_spec)
        operands.append(side[0])
        out_specs.append(dst_spec)
        out_shape.append(dst_shape)
    out = pl.pallas_call(
        functools.partial(_mm_norm_kernel, epilogue=epilogue, transposed=transposed, cfg=cfg,
                          n_side=len(sides)),
        grid=(n_j, n_i),
        in_specs=in_specs,
        out_specs=out_specs,
        out_shape=out_shape,
        scratch_shapes=[pltpu.VMEM(wb_shape, BF16)],
        compiler_params=_params(2),
        name=name,
    )(*operands)
    return out if sides else out[0]


def _mm_swiglu_kernel(x_ref, wg_ref, wu_ref, ssq_ref, side_src, o_ref, side_dst):
    _side_casts((side_src,), (side_dst,))
    r = _row_scale(x_ref, ssq_ref)
    tm = x_ref.shape[0]
    for r0 in range(0, tm, SWIGLU_ROWS):
        rows = slice(r0, r0 + SWIGLU_ROWS)
        x = x_ref[rows, :]
        gate = jnp.dot(x, wg_ref[...], preferred_element_type=F32) * r[rows]
        up = jnp.dot(x, wu_ref[...], preferred_element_type=F32) * r[rows]
        o_ref[rows, :] = (jax.nn.silu(gate) * up).astype(o_ref.dtype)


def _mm_swiglu(x, wg, wu, side, *, tm=2048, tn=512):
    xb, ssq = x
    T, K = xb.shape
    F = wg.shape[-1]
    n_j, n_i = F // tn, T // tm
    wspec = pl.BlockSpec((K, tn), lambda j, i: (0, j))
    src_spec, dst_spec, dst_shape = _side_cast_specs(side, n_j, n_i)
    return pl.pallas_call(
        _mm_swiglu_kernel,
        grid=(n_j, n_i),
        in_specs=[pl.BlockSpec((tm, K), lambda j, i: (i, 0)), wspec, wspec, _ssq_spec(ssq, tm),
                  src_spec],
        out_specs=[pl.BlockSpec((tm, tn), lambda j, i: (i, j)), dst_spec],
        out_shape=[jax.ShapeDtypeStruct((T, F), BF16), dst_shape],
        compiler_params=_params(2),
        name="mm_swiglu",
    )(xb, wg, wu, ssq, side[0])


def _row_ssq(x):
    return jnp.broadcast_to(jnp.sum(x * x, axis=-1, keepdims=True), (x.shape[0], LANES))


def _cast_stats_kernel(x_ref, xb_ref, ssq_ref):
    x = x_ref[...]
    xb_ref[...] = x.astype(BF16)
    ssq_ref[...] = _row_ssq(x)


def _cast_stats(x, tm=512):
    T, D = x.shape
    return pl.pallas_call(
        _cast_stats_kernel,
        grid=(T // tm,),
        in_specs=[pl.BlockSpec((tm, D), lambda i: (i, 0))],
        out_specs=[pl.BlockSpec((tm, D), lambda i: (i, 0)),
                   pl.BlockSpec((None, tm, LANES), lambda i: (0, i, 0))],
        out_shape=[jax.ShapeDtypeStruct((T, D), BF16), jax.ShapeDtypeStruct((1, T, LANES), F32)],
        compiler_params=_params(1),
        name="cast_stats",
    )(x)


def _mm_residual_kernel(a_ref, w_ref, r_ref, *refs, mode):
    if mode == "gain":
        gain_ref, refs = refs[0], refs[1:]
    o_ref = refs[0]
    x_new = r_ref[...] + jnp.dot(a_ref[...], w_ref[...], preferred_element_type=F32)
    o_ref[...] = x_new
    if mode != "last":
        ob_ref, ssq_ref = refs[1:]
        ob_ref[...] = (x_new * gain_ref[...] if mode == "gain" else x_new).astype(BF16)
        ssq_ref[...] = _row_ssq(x_new)


def _residual_tiles(K, N):
    return 512, (N if K <= 2048 else N // 2)


def _mm_residual(a, wb, res, *, gain=None, last=False):
    T, K = a.shape
    N = wb.shape[-1]
    tm, tn = _residual_tiles(K, N)
    tile = pl.BlockSpec((tm, tn), lambda j, i: (i, j))
    in_specs = [pl.BlockSpec((tm, K), lambda j, i: (i, 0)),
                pl.BlockSpec((K, tn), lambda j, i: (0, j)),
                tile]
    operands = [a, wb, res]
    mode = "last" if last else ("plain" if gain is None else "gain")
    if mode == "gain":
        g_all, g_layer = gain
        in_specs.append(pl.BlockSpec((None, 1, tn), lambda j, i: (g_layer, 0, j)))
        operands.append(g_all)
    out_specs = [tile, tile, pl.BlockSpec((None, tm, LANES), lambda j, i: (j, i, 0))]
    out_shape = [jax.ShapeDtypeStruct((T, N), F32), jax.ShapeDtypeStruct((T, N), BF16),
                 jax.ShapeDtypeStruct((N // tn, T, LANES), F32)]
    n_out = 1 if last else 3
    out = pl.pallas_call(
        functools.partial(_mm_residual_kernel, mode=mode),
        grid=(N // tn, T // tm),
        in_specs=in_specs,
        out_specs=out_specs[:n_out],
        out_shape=out_shape[:n_out],
        compiler_params=_params(2),
        name="mm_residual",
    )(*operands)
    return out[0], (None if last else (out[1], out[2]))


def _rope_kernel(pos_ref, invf_ref, cos_ref, sin_ref):
    ang = pos_ref[...].astype(F32) * invf_ref[...]
    cos_ref[...] = jnp.cos(ang)
    sin_ref[...] = jnp.sin(ang)


def _rope_tables(pos_col, inv_freq, tm=512):
    T = pos_col.shape[0]
    half = inv_freq.shape[-1]
    out = jax.ShapeDtypeStruct((T, half), F32)
    return pl.pallas_call(
        _rope_kernel,
        grid=(T // tm,),
        in_specs=[pl.BlockSpec((tm, 1), lambda i: (i, 0)),
                  pl.BlockSpec((1, half), lambda i: (0, 0))],
        out_specs=[pl.BlockSpec((tm, half), lambda i: (i, 0))] * 2,
        out_shape=[out, out],
        compiler_params=_params(1),
        name="rope_tables",
    )(pos_col, inv_freq)


def _retention_kernel(q_ref, kt_ref, vg_ref, din_ref, qd_ref, kd_ref, cd_ref, gn_ref,
                      y_ref, state_ref, *, heads, dk, dv):
    @pl.when(pl.program_id(1) == 0)
    def _():
        state_ref[...] = jnp.zeros_like(state_ref)

    g_off = heads * dv
    rows = q_ref.shape[0]
    for h in range(heads):
        qb = q_ref[:, h * dk:(h + 1) * dk]
        kt = kt_ref[h * dk:(h + 1) * dk, :]
        v = vg_ref[:, h * dv:(h + 1) * dv]
        scores = jnp.dot(qb, kt, preferred_element_type=F32) * din_ref[h]
        state = state_ref[h]
        q_dec = (qb.astype(F32) * qd_ref[h]).astype(BF16)
        o = (jnp.dot(scores.astype(BF16), v, preferred_element_type=F32)
             + jnp.dot(q_dec, state.astype(BF16), preferred_element_type=F32))
        kt_dec = (kt.astype(F32) * kd_ref[h]).astype(BF16)
        state_ref[h] = state * cd_ref[h] + jnp.dot(kt_dec, v, preferred_element_type=F32)
        gain = gn_ref[:, h * dv:(h + 1) * dv]
        for r0 in range(0, rows, NORM_ROWS):
            strip = o[r0:r0 + NORM_ROWS]
            sc = strip - jnp.mean(strip, axis=-1, keepdims=True)
            var = jnp.mean(sc * sc, axis=-1, keepdims=True)
            g = vg_ref[r0:r0 + NORM_ROWS, g_off + h * dv:g_off + (h + 1) * dv].astype(F32)
            y_ref[r0:r0 + NORM_ROWS, h * dv:(h + 1) * dv] = (
                jax.nn.silu(g) * (sc * lax.rsqrt(var + EPS) * gain)).astype(y_ref.dtype)


def _retention(q, kt, vg, gn_all, layer, batch, seq):
    T = q.shape[0]
    H, C = RET_HEADS, RET_BLOCK
    dk = q.shape[1] // H
    dv = vg.shape[1] // (2 * H)
    nc = seq // C
    log_gamma = jnp.log1p(-jnp.exp2(-5.0 - jnp.arange(H, dtype=F32)))
    idx = jnp.arange(C, dtype=F32)
    dist = idx[:, None] - idx[None, :]
    decay_in = jnp.where(dist >= 0,
                         jnp.exp(jnp.maximum(dist, 0.0)[None] * log_gamma[:, None, None]), 0.0)
    q_decay = jnp.exp((idx + 1.0)[None, :] * log_gamma[:, None])[..., None]
    k_decay = jnp.exp((C - 1.0 - idx)[None, :] * log_gamma[:, None])[:, None, :]
    chunk_decay = jnp.exp(C * log_gamma)[:, None, None]

    def whole(shape):
        return pl.BlockSpec(shape, lambda b, c: (0,) * len(shape))

    return pl.pallas_call(
        functools.partial(_retention_kernel, heads=H, dk=dk, dv=dv),
        grid=(batch, nc),
        in_specs=[pl.BlockSpec((C, H * dk), lambda b, c: (b * nc + c, 0)),
                  pl.BlockSpec((H * dk, C), lambda b, c: (0, b * nc + c)),
                  pl.BlockSpec((C, 2 * H * dv), lambda b, c: (b * nc + c, 0)),
                  whole((H, C, C)), whole((H, C, 1)), whole((H, 1, C)), whole((H, 1, 1)),
                  pl.BlockSpec((None, 1, H * dv), lambda b, c: (layer, 0, 0))],
        out_specs=pl.BlockSpec((C, H * dv), lambda b, c: (b * nc + c, 0)),
        out_shape=jax.ShapeDtypeStruct((T, H * dv), BF16),
        scratch_shapes=[pltpu.VMEM((H, dk, dv), F32)],
        compiler_params=_params(2),
        name="retention",
    )(q, kt, vg, decay_in, q_decay, k_decay, chunk_decay, gn_all)


def _sgu_kernel(u_ref, v_ref, lng_ref, lnb_ref, ws_ref, bs_ref, y_ref, *, groups):
    C = ws_ref.shape[-1]
    gw = v_ref.shape[1] // groups
    row = lax.broadcasted_iota(jnp.int32, (C, C), 0)
    col = lax.broadcasted_iota(jnp.int32, (C, C), 1)
    w_causal = [jnp.where(col <= row, ws_ref[g], 0.0).astype(BF16) for g in range(groups)]
    for c0 in range(0, v_ref.shape[0], C):
        rows = slice(c0, c0 + C)
        v = v_ref[rows, :].astype(F32)
        vc = v - jnp.mean(v, axis=-1, keepdims=True)
        var = jnp.mean(vc * vc, axis=-1, keepdims=True)
        vn = (vc * lax.rsqrt(var + EPS) * lng_ref[...] + lnb_ref[...]).astype(BF16)
        for g in range(groups):
            cols = slice(g * gw, (g + 1) * gw)
            mixed = jnp.dot(w_causal[g], vn[:, cols], preferred_element_type=F32)
            mixed = mixed + bs_ref[:, g:g + 1]
            y_ref[rows, cols] = (u_ref[rows, cols].astype(F32) * mixed).astype(y_ref.dtype)


def _sgu(z, lng_all, lnb_all, ws_all, bs_t, layer):
    T, dffn = z.shape
    half = dffn // 2
    G, C = GMLP_GROUPS, CHUNK
    R = 2 * C
    vec = pl.BlockSpec((None, 1, half), lambda i: (layer, 0, 0))
    return pl.pallas_call(
        functools.partial(_sgu_kernel, groups=G),
        grid=(T // R,),
        in_specs=[pl.BlockSpec((R, half), lambda i: (i, 0)),
                  pl.BlockSpec((R, half), lambda i: (i, 1)),
                  vec, vec,
                  pl.BlockSpec((None, G, C, C), lambda i: (layer, 0, 0, 0)),
                  pl.BlockSpec((None, C, G), lambda i: (layer, 0, 0))],
        out_specs=pl.BlockSpec((R, half), lambda i: (i, 0)),
        out_shape=jax.ShapeDtypeStruct((T, half), BF16),
        compiler_params=_params(1),
        name="sgu",
    )(z, z, lng_all, lnb_all, ws_all, bs_t)


def _fox_prep_kernel(f_ref, bf_ref, c3_ref, carry_ref, *, group):
    @pl.when(pl.program_id(1) == 0)
    def _():
        carry_ref[...] = jnp.zeros_like(carry_ref)

    log_f = jax.nn.log_sigmoid(f_ref[...] + bf_ref[...])
    tb, heads = log_f.shape
    row = lax.broadcasted_iota(jnp.int32, (tb, tb), 0)
    col = lax.broadcasted_iota(jnp.int32, (tb, tb), 1)
    tri = jnp.where(col <= row, 1.0, 0.0).astype(BF16)

    def split3(x):
        hi = x.astype(BF16)
        r1 = x - hi.astype(F32)
        mid = r1.astype(BF16)
        lo = (r1 - mid.astype(F32)).astype(BF16)
        return hi, mid, lo

    c = carry_ref[...] + sum(jnp.dot(tri, part, preferred_element_type=F32)
                             for part in split3(log_f))
    carry_ref[...] = c[tb - 1:tb, :]
    parts = [part.astype(F32) for part in split3(c * LOG2E)]
    pad = jnp.zeros((tb, LANES - 3 * group), F32)
    for g in range(heads // group):
        c3_ref[g] = jnp.concatenate(
            [part[:, g * group:(g + 1) * group] for part in parts] + [pad], axis=1)


def _fox_prep(f_logit, bf_all, layer, group, batch, seq, tb=512):
    T, H = f_logit.shape
    nb = seq // tb
    return pl.pallas_call(
        functools.partial(_fox_prep_kernel, group=group),
        grid=(batch, nb),
        in_specs=[pl.BlockSpec((tb, H), lambda b, s: (b * nb + s, 0)),
                  pl.BlockSpec((None, 1, H), lambda b, s: (layer, 0, 0))],
        out_specs=pl.BlockSpec((H // group, tb, LANES), lambda b, s: (0, b * nb + s, 0)),
        out_shape=jax.ShapeDtypeStruct((H // group, T, LANES), F32),
        scratch_shapes=[pltpu.VMEM((1, H), F32)],
        compiler_params=_params(2),
        name="fox_prep",
    )(f_logit, bf_all)


def _fox_attn_kernel(qi_tab, ki_tab, qa_ref, ka_ref, va_ref, g_ref, y_ref, m_ref, acc_ref,
                     *, heads, hd):
    p = pl.program_id(1)
    qi = qi_tab[p]
    ki = ki_tab[p]

    @pl.when(ki == 0)
    def _():
        m_ref[...] = jnp.full_like(m_ref, -jnp.inf)
        acc_ref[...] = jnp.zeros_like(acc_ref)

    tq = qa_ref.shape[0]

    def update(h, r0, r1, n_keys, masked):
        wide = slice(2 * h * hd, 2 * (h + 1) * hd)
        s = lax.dot_general(qa_ref[r0:r1, wide], ka_ref[0:n_keys, wide], (((1,), (1,)), ((), ())),
                            preferred_element_type=F32)
        if masked:
            row = lax.broadcasted_iota(jnp.int32, s.shape, 0) + r0
            col = lax.broadcasted_iota(jnp.int32, s.shape, 1)
            s = jnp.where(col <= row, s, -jnp.inf)
        m_prev = m_ref[h, r0:r1]
        m_new = jnp.maximum(m_prev, jnp.max(s, axis=-1, keepdims=True))
        alpha = jnp.exp2(m_prev - m_new)
        pexp = jnp.exp2(s - m_new[:, :1]).astype(BF16)
        pv = jnp.dot(pexp, va_ref[0:n_keys, wide], preferred_element_type=F32)
        acc_ref[h, r0:r1] = jnp.concatenate([alpha, alpha], axis=1) * acc_ref[h, r0:r1] + pv
        m_ref[h, r0:r1] = m_new

    @pl.when(ki < qi)
    def _():
        for h in range(heads):
            update(h, 0, tq, tq, False)

    @pl.when(ki == qi)
    def _():
        for h in range(heads):
            update(h, 0, tq // 2, tq // 2, True)
            update(h, tq // 2, tq, tq, True)
        for h in range(heads):
            sl = slice(h * hd, (h + 1) * hd)
            acc = acc_ref[h]
            o = acc[:, :hd] / acc[:, hd:hd + 1]
            y_ref[:, sl] = (jax.nn.sigmoid(g_ref[:, sl].astype(F32)) * o).astype(y_ref.dtype)


def _fox_attention(qa, ka, va, gate, batch, seq, tq=512):
    T, width = gate.shape
    H = FOX_HEADS
    hd = width // H
    nq = seq // tq
    pairs = [(qi, ki) for qi in range(nq) for ki in range(qi + 1)]
    qi_tab = jnp.asarray([p[0] for p in pairs], jnp.int32)
    ki_tab = jnp.asarray([p[1] for p in pairs], jnp.int32)

    def q_side(cols):
        return pl.BlockSpec((tq, cols), lambda b, p, qt, kt: (b * nq + qt[p], 0))

    def k_side(cols):
        return pl.BlockSpec((tq, cols), lambda b, p, qt, kt: (b * nq + kt[p], 0))

    grid_spec = pltpu.PrefetchScalarGridSpec(
        num_scalar_prefetch=2,
        grid=(batch, len(pairs)),
        in_specs=[q_side(2 * width), k_side(2 * width), k_side(2 * width), q_side(width)],
        out_specs=q_side(width),
        scratch_shapes=[pltpu.VMEM((H, tq, LANES), F32), pltpu.VMEM((H, tq, 2 * hd), F32)],
    )
    return pl.pallas_call(
        functools.partial(_fox_attn_kernel, heads=H, hd=hd),
        grid_spec=grid_spec,
        out_shape=jax.ShapeDtypeStruct((T, width), BF16),
        compiler_params=_params(2),
        name="fox_attention",
    )(qi_tab, ki_tab, qa, ka, va, gate)


def kernel(x, positions, mix_norm_g, ffn_norm_g, ret_w_in, ret_gn_g, ret_w_out, gmlp_w_in, gmlp_ln_g, gmlp_ln_b, gmlp_w_s, gmlp_b_s, gmlp_w_out, fox_w_in, fox_b_f, fox_qn_g, fox_kn_g, fox_w_out, ffn_w_gate, ffn_w_up, ffn_w_down):
    B, S, D = x.shape
    T = B * S
    depth = mix_norm_g.shape[0]
    xf = x.reshape(T, D)
    xn = _cast_stats(xf)

    def rows(a):
        return a.reshape(a.shape[0], 1, a.shape[1])

    mix_g = mix_norm_g[..., None]
    mix_g_row, ffn_g_row = rows(mix_norm_g), rows(ffn_norm_g)
    ret_gn, lng, lnb = rows(ret_gn_g), rows(gmlp_ln_g), rows(gmlp_ln_b)
    qn_g, kn_g, fox_bf = rows(fox_qn_g), rows(fox_kn_g), rows(fox_b_f)
    bs_t = jnp.swapaxes(gmlp_b_s, 1, 2)
    fox_w_t = jnp.swapaxes(fox_w_in, 1, 2)

    tm = tn = 1024
    ret_dk = ret_w_in.shape[-1] // (6 * RET_HEADS)
    ret_qk = 2 * RET_HEADS * ret_dk
    inv_freq = (ROPE_BASE ** (-jnp.arange(ret_dk // 2, dtype=F32) / (ret_dk // 2))).reshape(1, -1)
    cos, sin = _rope_tables(positions.reshape(T, 1), inv_freq)
    rope_spec = pl.BlockSpec((tm, ret_dk // 2), lambda j, i: (i, 0))

    hd = fox_qn_g.shape[-1]
    fox_d = FOX_HEADS * hd
    for i in range(depth):
        kind, j = i % N_MIXERS, i // N_MIXERS
        if kind == 0:
            rope = dict(extra=(cos, sin), extra_specs=(rope_spec, rope_spec), cfg=(ret_dk,),
                        tm=tm, tn=tn)
            q, w_gate = _mm_norm(xn, mix_g, ret_w_in, j, i, ret_qk // 2, epilogue="rope_q",
                                 sides=[(ffn_w_gate, i)], name="mm_ret_q", **rope)
            kt, w_up = _mm_norm(xn, mix_g, ret_w_in, j, i, ret_qk // 2, col0=ret_qk // 2 // tn,
                                epilogue="rope_kt", sides=[(ffn_w_up, i)], name="mm_ret_k", **rope)
            vg, w_out = _mm_norm(xn, mix_g, ret_w_in, j, i, ret_w_in.shape[-1] - ret_qk,
                                 col0=ret_qk // tn, sides=[(ret_w_out, j)], tm=tm, tn=tn,
                                 name="mm_ret_vg")
            y = _retention(q, kt, vg, ret_gn, j, B, S)
        elif kind == 1:
            z, w_out, w_gate, w_up = _mm_norm(
                xn, mix_g, gmlp_w_in, j, i, gmlp_w_in.shape[-1], epilogue="gelu",
                sides=[(gmlp_w_out, j), (ffn_w_gate, i), (ffn_w_up, i)], name="mm_gmlp_in")
            y = _sgu(z, lng, lnb, gmlp_w_s, bs_t, j)
        else:
            f_logit = _mm_norm(xn, mix_g_row, fox_w_t, j, i, FOX_HEADS,
                               col0=4 * fox_d // FOX_HEADS, transposed=True, out_dtype=F32,
                               tm=tm, tn=FOX_HEADS, name="mm_fox_forget")
            c3 = _fox_prep(f_logit, fox_bf, j, tn // hd, B, S)
            gain_spec = pl.BlockSpec((None, 1, hd), lambda jj, ii: (j, 0, 0))
            c3_spec = pl.BlockSpec((None, tm, LANES), lambda jj, ii: (jj, ii, 0))
            qa, w_gate = _mm_norm(
                xn, mix_g_row, fox_w_t, j, i, fox_d, transposed=True, epilogue="fox_q",
                extra=(qn_g, c3), extra_specs=(gain_spec, c3_spec), cfg=(hd, hd ** -0.5 * LOG2E),
                sides=[(ffn_w_gate, i)], tm=tm, tn=tn, name="mm_fox_q")
            ka, w_up = _mm_norm(
                xn, mix_g_row, fox_w_t, j, i, fox_d, col0=fox_d // tn, transposed=True,
                epilogue="fox_k", extra=(kn_g, c3), extra_specs=(gain_spec, c3_spec), cfg=(hd, 1.0),
                sides=[(ffn_w_up, i)], tm=tm, tn=tn, name="mm_fox_k")
            va = _mm_norm(xn, mix_g_row, fox_w_t, j, i, fox_d, col0=2 * fox_d // tn,
                          transposed=True, epilogue="fox_v", cfg=(hd,), tm=tm, tn=tn,
                          name="mm_fox_v")
            gate, w_out = _mm_norm(xn, mix_g_row, fox_w_t, j, i, fox_d, col0=3 * fox_d // tn,
                                   transposed=True, sides=[(fox_w_out, j)], tm=tm, tn=tn,
                                   name="mm_fox_gate")
            y = _fox_attention(qa, ka, va, gate, B, S)
        xf, xn = _mm_residual(y, w_out, xf, gain=(ffn_g_row, i))
        a, w_down = _mm_swiglu(xn, w_gate, w_up, (ffn_w_down, i))
        xf, xn = _mm_residual(a, w_down, xf, last=i == depth - 1)
    return xf.reshape(B, S, D)
```

```python
import functools

import jax
import jax.numpy as jnp
from jax import lax
from jax.experimental import pallas as pl
from jax.experimental.pallas import tpu as pltpu

F32 = jnp.float32
BF16 = jnp.bfloat16

EPS = 1e-6
CHUNK = 128
RET_BLOCK = 256
NORM_ROWS = 32
SWIGLU_ROWS = 1024
MM_SLAB_ROWS = 512
ROPE_BASE = 10000.0
RET_HEADS = 8
GMLP_GROUPS = 8
FOX_HEADS = 16
N_MIXERS = 3
LOG2E = 1.4426950408889634

LANES = 128
VMEM_LIMIT = 56 * 1024 * 1024


def _params(n_axes):
    return pltpu.CompilerParams(
        dimension_semantics=("arbitrary",) * n_axes, vmem_limit_bytes=VMEM_LIMIT)


def _gelu_exact(x):
    return 0.5 * x * (1.0 + lax.erf(x * (2.0 ** -0.5)))


def _row_scale(x_ref, ssq_ref):
    ssq = jnp.sum(ssq_ref[...], axis=0)[:, :1]
    return lax.rsqrt(ssq * (1.0 / x_ref.shape[1]) + EPS)


def _side_casts(src_refs, dst_refs):
    for src_ref, dst_ref in zip(src_refs, dst_refs):
        dst_ref[...] = src_ref[...].astype(BF16)


def _side_cast_specs(side, n_j, n_i):
    w_all, layer = side
    K, N = w_all.shape[1:]
    n_steps = n_j * n_i
    rc = next(r for r in range(16, K + 1, 16) if K % r == 0 and K // r <= n_steps)
    last = K // rc - 1
    return (pl.BlockSpec((None, rc, N), lambda j, i: (layer, jnp.minimum(j * n_i + i, last), 0)),
            pl.BlockSpec((rc, N), lambda j, i: (jnp.minimum(j * n_i + i, last), 0)),
            jax.ShapeDtypeStruct((K, N), BF16))


def _mm_norm_kernel(*refs, epilogue, transposed, cfg, n_side):
    x_ref, g_ref, w_ref, ssq_ref = refs[:4]
    wb_ref = refs[-1]
    o_ref = refs[-2 - n_side]
    extra = refs[4:len(refs) - 2 - 2 * n_side]
    _side_casts(refs[len(refs) - 2 - 2 * n_side:len(refs) - 2 - n_side], refs[len(refs) - 1 - n_side:-1])

    @pl.when(pl.program_id(1) == 0)
    def _():
        wb_ref[...] = (w_ref[...] * g_ref[...]).astype(BF16)

    r = _row_scale(x_ref, ssq_ref)
    slab = x_ref.shape[0] if epilogue is None else MM_SLAB_ROWS
    for r0 in range(0, x_ref.shape[0], slab):
        _mm_norm_slab(slice(r0, r0 + slab), x_ref, wb_ref, r, extra, o_ref,
                      epilogue=epilogue, transposed=transposed, cfg=cfg)


def _mm_norm_slab(rows, x_ref, wb_ref, r, extra, o_ref, *, epilogue, transposed, cfg):
    if transposed:
        acc = lax.dot_general(x_ref[rows, :], wb_ref[...], (((1,), (1,)), ((), ())),
                              preferred_element_type=F32) * r[rows]
    else:
        acc = jnp.dot(x_ref[rows, :], wb_ref[...], preferred_element_type=F32) * r[rows]
    if epilogue == "gelu":
        o_ref[rows, :] = _gelu_exact(acc).astype(o_ref.dtype)
    elif epilogue in ("rope_q", "rope_kt"):
        cos_ref, sin_ref = extra
        dk = cfg[0]
        half = dk // 2
        scale = 1.0 if epilogue == "rope_q" else dk ** -0.5
        cos = cos_ref[rows, :] * scale
        sin = sin_ref[rows, :] * scale
        for h in range(acc.shape[1] // dk):
            t1 = acc[:, h * dk:h * dk + half]
            t2 = acc[:, h * dk + half:(h + 1) * dk]
            r1, r2 = t1 * cos - t2 * sin, t2 * cos + t1 * sin
            if epilogue == "rope_q":
                o_ref[rows, h * dk:h * dk + half] = r1.astype(o_ref.dtype)
                o_ref[rows, h * dk + half:(h + 1) * dk] = r2.astype(o_ref.dtype)
            else:
                o_ref[h * dk:h * dk + half, rows] = r1.T.astype(o_ref.dtype)
                o_ref[h * dk + half:(h + 1) * dk, rows] = r2.T.astype(o_ref.dtype)
    elif epilogue in ("fox_q", "fox_k", "fox_v"):
        hd = cfg[0]
        heads = acc.shape[1] // hd
        lane = lax.broadcasted_iota(jnp.int32, (acc.shape[0], hd), 1)
        if epilogue == "fox_v":
            ones_col = jnp.where(lane == 0, 1.0, 0.0).astype(o_ref.dtype)
        else:
            gain_ref, c3_ref = extra
            gain = gain_ref[...] * cfg[1]
        for h in range(heads):
            t = acc[:, h * hd:(h + 1) * hd]
            if epilogue == "fox_v":
                data, aug = t, ones_col
            else:
                data = t * lax.rsqrt(jnp.mean(t * t, axis=-1, keepdims=True) + EPS) * gain
                hi, mid, lo = (c3_ref[rows, p * heads + h:p * heads + h + 1] for p in range(3))
                if epilogue == "fox_q":
                    aug = jnp.where(lane == 0, hi, jnp.where(lane == 1, mid, jnp.where(
                        lane == 2, lo, jnp.where(lane < 6, 1.0, 0.0))))
                else:
                    aug = jnp.where(lane < 3, 1.0, jnp.where(lane == 3, -hi, jnp.where(
                        lane == 4, -mid, jnp.where(lane == 5, -lo, 0.0))))
            o_ref[rows, 2 * h * hd:(2 * h + 1) * hd] = data.astype(o_ref.dtype)
            o_ref[rows, (2 * h + 1) * hd:(2 * h + 2) * hd] = aug.astype(o_ref.dtype)
    else:
        o_ref[rows, :] = acc.astype(o_ref.dtype)


def _ssq_spec(ssq, tm):
    return pl.BlockSpec((ssq.shape[0], tm, LANES), lambda j, i: (0, i, 0))


def _mm_norm(x, gain, w_all, layer, g_layer, n_out, *, col0=0, transposed=False, epilogue=None,
             extra=(), extra_specs=(), cfg=None, sides=(), out_dtype=BF16, tm=1024, tn=1024,
             name="mm_norm"):
    xb, ssq = x
    T, K = xb.shape
    n_j, n_i = n_out // tn, T // tm
    if transposed:
        g_spec = pl.BlockSpec((None, 1, K), lambda j, i: (g_layer, 0, 0))
        w_spec = pl.BlockSpec((None, tn, K), lambda j, i: (layer, col0 + j, 0))
        wb_shape = (tn, K)
    else:
        g_spec = pl.BlockSpec((None, K, 1), lambda j, i: (g_layer, 0, 0))
        w_spec = pl.BlockSpec((None, K, tn), lambda j, i: (layer, 0, col0 + j))
        wb_shape = (K, tn)
    in_specs = [pl.BlockSpec((tm, K), lambda j, i: (i, 0)), g_spec, w_spec, _ssq_spec(ssq, tm),
                *extra_specs]
    operands = [xb, gain, w_all, ssq, *extra]
    widen = 2 if epilogue in ("fox_q", "fox_k", "fox_v") else 1
    if epilogue == "rope_kt":
        out_specs = [pl.BlockSpec((widen * tn, tm), lambda j, i: (j, i))]
        out_shape = [jax.ShapeDtypeStruct((widen * n_out, T), out_dtype)]
    else:
        out_specs = [pl.BlockSpec((tm, widen * tn), lambda j, i: (i, j))]
        out_shape = [jax.ShapeDtypeStruct((T, widen * n_out), out_dtype)]
    for side in sides:
        src_spec, dst_spec, dst_shape = _side_cast_specs(side, n_j, n_i)
        in_specs.append(src_spec)
        operands.append(side[0])
        out_specs.append(dst_spec)
        out_shape.append(dst_shape)
    out = pl.pallas_call(
        functools.partial(_mm_norm_kernel, epilogue=epilogue, transposed=transposed, cfg=cfg,
                          n_side=len(sides)),
        grid=(n_j, n_i),
        in_specs=in_specs,
        out_specs=out_specs,
        out_shape=out_shape,
        scratch_shapes=[pltpu.VMEM(wb_shape, BF16)],
        compiler_params=_params(2),
        name=name,
    )(*operands)
    return out if sides else out[0]


def _mm_swiglu_kernel(x_ref, wg_ref, wu_ref, ssq_ref, side_src, o_ref, side_dst):
    _side_casts((side_src,), (side_dst,))
    r = _row_scale(x_ref, ssq_ref)
    tm = x_ref.shape[0]
    for r0 in range(0, tm, SWIGLU_ROWS):
        rows = slice(r0, r0 + SWIGLU_ROWS)
        x = x_ref[rows, :]
        gate = jnp.dot(x, wg_ref[...], preferred_element_type=F32) * r[rows]
        up = jnp.dot(x, wu_ref[...], preferred_element_type=F32) * r[rows]
        o_ref[rows, :] = (jax.nn.silu(gate) * up).astype(o_ref.dtype)


def _mm_swiglu(x, wg, wu, side, *, tm=2048, tn=512):
    xb, ssq = x
    T, K = xb.shape
    F = wg.shape[-1]
    n_j, n_i = F // tn, T // tm
    wspec = pl.BlockSpec((K, tn), lambda j, i: (0, j))
    src_spec, dst_spec, dst_shape = _side_cast_specs(side, n_j, n_i)
    return pl.pallas_call(
        _mm_swiglu_kernel,
        grid=(n_j, n_i),
        in_specs=[pl.BlockSpec((tm, K), lambda j, i: (i, 0)), wspec, wspec, _ssq_spec(ssq, tm),
                  src_spec],
        out_specs=[pl.BlockSpec((tm, tn), lambda j, i: (i, j)), dst_spec],
        out_shape=[jax.ShapeDtypeStruct((T, F), BF16), dst_shape],
        compiler_params=_params(2),
        name="mm_swiglu",
    )(xb, wg, wu, ssq, side[0])


def _row_ssq(x):
    return jnp.broadcast_to(jnp.sum(x * x, axis=-1, keepdims=True), (x.shape[0], LANES))


def _entry_kernel(x_ref, pos_ref, invf_ref, xb_ref, ssq_ref, cos_ref, sin_ref):
    x = x_ref[...]
    xb_ref[...] = x.astype(BF16)
    ssq_ref[...] = _row_ssq(x)
    ang = pos_ref[...].astype(F32) * invf_ref[...]
    cos_ref[...] = jnp.cos(ang)
    sin_ref[...] = jnp.sin(ang)


def _entry(x, pos_col, inv_freq, tm=512):
    T, D = x.shape
    half = inv_freq.shape[-1]
    rows = pl.BlockSpec((tm, half), lambda i: (i, 0))
    table = jax.ShapeDtypeStruct((T, half), F32)
    xb, ssq, cos, sin = pl.pallas_call(
        _entry_kernel,
        grid=(T // tm,),
        in_specs=[pl.BlockSpec((tm, D), lambda i: (i, 0)),
                  pl.BlockSpec((tm, 1), lambda i: (i, 0)),
                  pl.BlockSpec((1, half), lambda i: (0, 0))],
        out_specs=[pl.BlockSpec((tm, D), lambda i: (i, 0)),
                   pl.BlockSpec((None, tm, LANES), lambda i: (0, i, 0)), rows, rows],
        out_shape=[jax.ShapeDtypeStruct((T, D), BF16), jax.ShapeDtypeStruct((1, T, LANES), F32),
                   table, table],
        compiler_params=_params(1),
        name="entry",
    )(x, pos_col, inv_freq)
    return (xb, ssq), cos, sin


def _mm_residual_kernel(a_ref, w_ref, r_ref, *refs, mode):
    if mode == "gain":
        gain_ref, refs = refs[0], refs[1:]
    o_ref = refs[0]
    x_new = r_ref[...] + jnp.dot(a_ref[...], w_ref[...], preferred_element_type=F32)
    o_ref[...] = x_new
    if mode != "last":
        ob_ref, ssq_ref = refs[1:]
        ob_ref[...] = (x_new * gain_ref[...] if mode == "gain" else x_new).astype(BF16)
        ssq_ref[...] = _row_ssq(x_new)


def _residual_tiles(K, N):
    return 512, (N if K <= 2048 else N // 2)


def _mm_residual(a, wb, res, *, gain=None, last=False):
    T, K = a.shape
    N = wb.shape[-1]
    tm, tn = _residual_tiles(K, N)
    tile = pl.BlockSpec((tm, tn), lambda j, i: (i, j))
    in_specs = [pl.BlockSpec((tm, K), lambda j, i: (i, 0)),
                pl.BlockSpec((K, tn), lambda j, i: (0, j)),
                tile]
    operands = [a, wb, res]
    mode = "last" if last else ("plain" if gain is None else "gain")
    if mode == "gain":
        g_all, g_layer = gain
        in_specs.append(pl.BlockSpec((None, 1, tn), lambda j, i: (g_layer, 0, j)))
        operands.append(g_all)
    out_specs = [tile, tile, pl.BlockSpec((None, tm, LANES), lambda j, i: (j, i, 0))]
    out_shape = [jax.ShapeDtypeStruct((T, N), F32), jax.ShapeDtypeStruct((T, N), BF16),
                 jax.ShapeDtypeStruct((N // tn, T, LANES), F32)]
    n_out = 1 if last else 3
    out = pl.pallas_call(
        functools.partial(_mm_residual_kernel, mode=mode),
        grid=(N // tn, T // tm),
        in_specs=in_specs,
        out_specs=out_specs[:n_out],
        out_shape=out_shape[:n_out],
        compiler_params=_params(2),
        name="mm_residual",
    )(*operands)
    return out[0], (None if last else (out[1], out[2]))


def _retention_kernel(q_ref, kt_ref, vg_ref, din_ref, qd_ref, kd_ref, cd_ref, gn_ref,
                      y_ref, state_ref, *, heads, dk, dv):
    @pl.when(pl.program_id(1) == 0)
    def _():
        state_ref[...] = jnp.zeros_like(state_ref)

    g_off = heads * dv
    rows = q_ref.shape[0]
    for h in range(heads):
        qb = q_ref[:, h * dk:(h + 1) * dk]
        kt = kt_ref[h * dk:(h + 1) * dk, :]
        v = vg_ref[:, h * dv:(h + 1) * dv]
        scores = jnp.dot(qb, kt, preferred_element_type=F32) * din_ref[h]
        state = state_ref[h]
        q_dec = (qb.astype(F32) * qd_ref[h]).astype(BF16)
        o = (jnp.dot(scores.astype(BF16), v, preferred_element_type=F32)
             + jnp.dot(q_dec, state.astype(BF16), preferred_element_type=F32))
        kt_dec = (kt.astype(F32) * kd_ref[h]).astype(BF16)
        state_ref[h] = state * cd_ref[h] + jnp.dot(kt_dec, v, preferred_element_type=F32)
        gain = gn_ref[:, h * dv:(h + 1) * dv]
        for r0 in range(0, rows, NORM_ROWS):
            strip = o[r0:r0 + NORM_ROWS]
            sc = strip - jnp.mean(strip, axis=-1, keepdims=True)
            var = jnp.mean(sc * sc, axis=-1, keepdims=True)
            g = vg_ref[r0:r0 + NORM_ROWS, g_off + h * dv:g_off + (h + 1) * dv].astype(F32)
            y_ref[r0:r0 + NORM_ROWS, h * dv:(h + 1) * dv] = (
                jax.nn.silu(g) * (sc * lax.rsqrt(var + EPS) * gain)).astype(y_ref.dtype)


def _retention(q, kt, vg, gn_all, layer, batch, seq):
    T = q.shape[0]
    H, C = RET_HEADS, RET_BLOCK
    dk = q.shape[1] // H
    dv = vg.shape[1] // (2 * H)
    nc = seq // C
    log_gamma = jnp.log1p(-jnp.exp2(-5.0 - jnp.arange(H, dtype=F32)))
    idx = jnp.arange(C, dtype=F32)
    dist = idx[:, None] - idx[None, :]
    decay_in = jnp.where(dist >= 0,
                         jnp.exp(jnp.maximum(dist, 0.0)[None] * log_gamma[:, None, None]), 0.0)
    q_decay = jnp.exp((idx + 1.0)[None, :] * log_gamma[:, None])[..., None]
    k_decay = jnp.exp((C - 1.0 - idx)[None, :] * log_gamma[:, None])[:, None, :]
    chunk_decay = jnp.exp(C * log_gamma)[:, None, None]

    def whole(shape):
        return pl.BlockSpec(shape, lambda b, c: (0,) * len(shape))

    return pl.pallas_call(
        functools.partial(_retention_kernel, heads=H, dk=dk, dv=dv),
        grid=(batch, nc),
        in_specs=[pl.BlockSpec((C, H * dk), lambda b, c: (b * nc + c, 0)),
                  pl.BlockSpec((H * dk, C), lambda b, c: (0, b * nc + c)),
                  pl.BlockSpec((C, 2 * H * dv), lambda b, c: (b * nc + c, 0)),
                  whole((H, C, C)), whole((H, C, 1)), whole((H, 1, C)), whole((H, 1, 1)),
                  pl.BlockSpec((None, 1, H * dv), lambda b, c: (layer, 0, 0))],
        out_specs=pl.BlockSpec((C, H * dv), lambda b, c: (b * nc + c, 0)),
        out_shape=jax.ShapeDtypeStruct((T, H * dv), BF16),
        scratch_shapes=[pltpu.VMEM((H, dk, dv), F32)],
        compiler_params=_params(2),
        name="retention",
    )(q, kt, vg, decay_in, q_decay, k_decay, chunk_decay, gn_all)


def _sgu_kernel(u_ref, v_ref, lng_ref, lnb_ref, ws_ref, bs_ref, y_ref, *, groups):
    C = ws_ref.shape[-1]
    gw = v_ref.shape[1] // groups
    row = lax.broadcasted_iota(jnp.int32, (C, C), 0)
    col = lax.broadcasted_iota(jnp.int32, (C, C), 1)
    w_causal = [jnp.where(col <= row, ws_ref[g], 0.0).astype(BF16) for g in range(groups)]
    for c0 in range(0, v_ref.shape[0], C):
        rows = slice(c0, c0 + C)
        v = v_ref[rows, :].astype(F32)
        vc = v - jnp.mean(v, axis=-1, keepdims=True)
        var = jnp.mean(vc * vc, axis=-1, keepdims=True)
        vn = (vc * lax.rsqrt(var + EPS) * lng_ref[...] + lnb_ref[...]).astype(BF16)
        for g in range(groups):
            cols = slice(g * gw, (g + 1) * gw)
            mixed = jnp.dot(w_causal[g], vn[:, cols], preferred_element_type=F32)
            mixed = mixed + bs_ref[:, g:g + 1]
            y_ref[rows, cols] = (u_ref[rows, cols].astype(F32) * mixed).astype(y_ref.dtype)


def _sgu(z, lng_all, lnb_all, ws_all, bs_t, layer):
    T, dffn = z.shape
    half = dffn // 2
    G, C = GMLP_GROUPS, CHUNK
    R = 2 * C
    vec = pl.BlockSpec((None, 1, half), lambda i: (layer, 0, 0))
    return pl.pallas_call(
        functools.partial(_sgu_kernel, groups=G),
        grid=(T // R,),
        in_specs=[pl.BlockSpec((R, half), lambda i: (i, 0)),
                  pl.BlockSpec((R, half), lambda i: (i, 1)),
                  vec, vec,
                  pl.BlockSpec((None, G, C, C), lambda i: (layer, 0, 0, 0)),
                  pl.BlockSpec((None, C, G), lambda i: (layer, 0, 0))],
        out_specs=pl.BlockSpec((R, half), lambda i: (i, 0)),
        out_shape=jax.ShapeDtypeStruct((T, half), BF16),
        compiler_params=_params(1),
        name="sgu",
    )(z, z, lng_all, lnb_all, ws_all, bs_t)


def _fox_prep_kernel(x_ref, g_ref, wf_ref, ssq_ref, bf_ref, c3_ref, wfb_ref, carry_ref, *, group):
    @pl.when(pl.program_id(1) == 0)
    def _():
        carry_ref[...] = jnp.zeros_like(carry_ref)
        wfb_ref[...] = (wf_ref[...] * g_ref[...]).astype(BF16)

    f_logit = lax.dot_general(x_ref[...], wfb_ref[...], (((1,), (1,)), ((), ())),
                              preferred_element_type=F32) * _row_scale(x_ref, ssq_ref)
    log_f = jax.nn.log_sigmoid(f_logit + bf_ref[...])
    tb, heads = log_f.shape
    row = lax.broadcasted_iota(jnp.int32, (tb, tb), 0)
    col = lax.broadcasted_iota(jnp.int32, (tb, tb), 1)
    tri = jnp.where(col <= row, 1.0, 0.0).astype(BF16)

    def split3(x):
        hi = x.astype(BF16)
        r1 = x - hi.astype(F32)
        mid = r1.astype(BF16)
        lo = (r1 - mid.astype(F32)).astype(BF16)
        return hi, mid, lo

    c = carry_ref[...] + sum(jnp.dot(tri, part, preferred_element_type=F32)
                             for part in split3(log_f))
    carry_ref[...] = c[tb - 1:tb, :]
    parts = [part.astype(F32) for part in split3(c * LOG2E)]
    pad = jnp.zeros((tb, LANES - 3 * group), F32)
    for g in range(heads // group):
        c3_ref[g] = jnp.concatenate(
            [part[:, g * group:(g + 1) * group] for part in parts] + [pad], axis=1)


def _fox_prep(x, gain_row, w_t, layer, g_layer, row0, bf_all, group, batch, seq, tb=512):
    xb, ssq = x
    T, K = xb.shape
    H = bf_all.shape[-1]
    nb = seq // tb
    return pl.pallas_call(
        functools.partial(_fox_prep_kernel, group=group),
        grid=(batch, nb),
        in_specs=[pl.BlockSpec((tb, K), lambda b, s: (b * nb + s, 0)),
                  pl.BlockSpec((None, 1, K), lambda b, s: (g_layer, 0, 0)),
                  pl.BlockSpec((None, H, K), lambda b, s: (layer, row0 // H, 0)),
                  pl.BlockSpec((ssq.shape[0], tb, LANES), lambda b, s: (0, b * nb + s, 0)),
                  pl.BlockSpec((None, 1, H), lambda b, s: (layer, 0, 0))],
        out_specs=pl.BlockSpec((H // group, tb, LANES), lambda b, s: (0, b * nb + s, 0)),
        out_shape=jax.ShapeDtypeStruct((H // group, T, LANES), F32),
        scratch_shapes=[pltpu.VMEM((H, K), BF16), pltpu.VMEM((1, H), F32)],
        compiler_params=_params(2),
        name="fox_prep",
    )(xb, gain_row, w_t, ssq, bf_all)


def _fox_attn_kernel(qi_tab, ki_tab, qa_ref, ka_ref, va_ref, g_ref, y_ref, m_ref, acc_ref,
                     *, heads, hd):
    p = pl.program_id(1)
    qi = qi_tab[p]
    ki = ki_tab[p]

    @pl.when(ki == 0)
    def _():
        m_ref[...] = jnp.full_like(m_ref, -jnp.inf)
        acc_ref[...] = jnp.zeros_like(acc_ref)

    tq = qa_ref.shape[0]

    def update(h, r0, r1, n_keys, masked):
        wide = slice(2 * h * hd, 2 * (h + 1) * hd)
        s = lax.dot_general(qa_ref[r0:r1, wide], ka_ref[0:n_keys, wide], (((1,), (1,)), ((), ())),
                            preferred_element_type=F32)
        if masked:
            row = lax.broadcasted_iota(jnp.int32, s.shape, 0) + r0
            col = lax.broadcasted_iota(jnp.int32, s.shape, 1)
            s = jnp.where(col <= row, s, -jnp.inf)
        m_prev = m_ref[h, r0:r1]
        m_new = jnp.maximum(m_prev, jnp.max(s, axis=-1, keepdims=True))
        alpha = jnp.exp2(m_prev - m_new)
        pexp = jnp.exp2(s - m_new[:, :1]).astype(BF16)
        pv = jnp.dot(pexp, va_ref[0:n_keys, wide], preferred_element_type=F32)
        acc_ref[h, r0:r1] = jnp.concatenate([alpha, alpha], axis=1) * acc_ref[h, r0:r1] + pv
        m_ref[h, r0:r1] = m_new

    @pl.when(ki < qi)
    def _():
        for h in range(heads):
            update(h, 0, tq, tq, False)

    @pl.when(ki == qi)
    def _():
        for h in range(heads):
            update(h, 0, tq // 2, tq // 2, True)
            update(h, tq // 2, tq, tq, True)
        for h in range(heads):
            sl = slice(h * hd, (h + 1) * hd)
            acc = acc_ref[h]
            o = acc[:, :hd] / acc[:, hd:hd + 1]
            y_ref[:, sl] = (jax.nn.sigmoid(g_ref[:, sl].astype(F32)) * o).astype(y_ref.dtype)


def _fox_attention(qa, ka, va, gate, batch, seq, tq=512):
    T, width = gate.shape
    H = FOX_HEADS
    hd = width // H
    nq = seq // tq
    pairs = [(qi, ki) for qi in range(nq) for ki in range(qi + 1)]
    qi_tab = jnp.asarray([p[0] for p in pairs], jnp.int32)
    ki_tab = jnp.asarray([p[1] for p in pairs], jnp.int32)

    def q_side(cols):
        return pl.BlockSpec((tq, cols), lambda b, p, qt, kt: (b * nq + qt[p], 0))

    def k_side(cols):
        return pl.BlockSpec((tq, cols), lambda b, p, qt, kt: (b * nq + kt[p], 0))

    grid_spec = pltpu.PrefetchScalarGridSpec(
        num_scalar_prefetch=2,
        grid=(batch, len(pairs)),
        in_specs=[q_side(2 * width), k_side(2 * width), k_side(2 * width), q_side(width)],
        out_specs=q_side(width),
        scratch_shapes=[pltpu.VMEM((H, tq, LANES), F32), pltpu.VMEM((H, tq, 2 * hd), F32)],
    )
    return pl.pallas_call(
        functools.partial(_fox_attn_kernel, heads=H, hd=hd),
        grid_spec=grid_spec,
        out_shape=jax.ShapeDtypeStruct((T, width), BF16),
        compiler_params=_params(2),
        name="fox_attention",
    )(qi_tab, ki_tab, qa, ka, va, gate)


def kernel(x, positions, mix_norm_g, ffn_norm_g, ret_w_in, ret_gn_g, ret_w_out, gmlp_w_in, gmlp_ln_g, gmlp_ln_b, gmlp_w_s, gmlp_b_s, gmlp_w_out, fox_w_in, fox_b_f, fox_qn_g, fox_kn_g, fox_w_out, ffn_w_gate, ffn_w_up, ffn_w_down):
    B, S, D = x.shape
    T = B * S
    depth = mix_norm_g.shape[0]
    xf = x.reshape(T, D)

    def rows(a):
        return a.reshape(a.shape[0], 1, a.shape[1])

    mix_g = mix_norm_g[..., None]
    mix_g_row, ffn_g_row = rows(mix_norm_g), rows(ffn_norm_g)
    ret_gn, lng, lnb = rows(ret_gn_g), rows(gmlp_ln_g), rows(gmlp_ln_b)
    qn_g, kn_g, fox_bf = rows(fox_qn_g), rows(fox_kn_g), rows(fox_b_f)
    bs_t = jnp.swapaxes(gmlp_b_s, 1, 2)
    fox_w_t = jnp.swapaxes(fox_w_in, 1, 2)

    tm = tn = 1024
    ret_dk = ret_w_in.shape[-1] // (6 * RET_HEADS)
    ret_qk = 2 * RET_HEADS * ret_dk
    inv_freq = (ROPE_BASE ** (-jnp.arange(ret_dk // 2, dtype=F32) / (ret_dk // 2))).reshape(1, -1)
    xn, cos, sin = _entry(xf, positions.reshape(T, 1), inv_freq)
    rope_spec = pl.BlockSpec((tm, ret_dk // 2), lambda j, i: (i, 0))

    hd = fox_qn_g.shape[-1]
    fox_d = FOX_HEADS * hd
    for i in range(depth):
        kind, j = i % N_MIXERS, i // N_MIXERS
        if kind == 0:
            rope = dict(extra=(cos, sin), extra_specs=(rope_spec, rope_spec), cfg=(ret_dk,),
                        tm=tm, tn=tn)
            q, w_gate = _mm_norm(xn, mix_g, ret_w_in, j, i, ret_qk // 2, epilogue="rope_q",
                                 sides=[(ffn_w_gate, i)], name="mm_ret_q", **rope)
            kt, w_up = _mm_norm(xn, mix_g, ret_w_in, j, i, ret_qk // 2, col0=ret_qk // 2 // tn,
                                epilogue="rope_kt", sides=[(ffn_w_up, i)], name="mm_ret_k", **rope)
            vg, w_out = _mm_norm(xn, mix_g, ret_w_in, j, i, ret_w_in.shape[-1] - ret_qk,
                                 col0=ret_qk // tn, sides=[(ret_w_out, j)], tm=tm, tn=tn,
                                 name="mm_ret_vg")
            y = _retention(q, kt, vg, ret_gn, j, B, S)
        elif kind == 1:
            z, w_out, w_gate, w_up = _mm_norm(
                xn, mix_g, gmlp_w_in, j, i, gmlp_w_in.shape[-1], epilogue="gelu",
                sides=[(gmlp_w_out, j), (ffn_w_gate, i), (ffn_w_up, i)], name="mm_gmlp_in")
            y = _sgu(z, lng, lnb, gmlp_w_s, bs_t, j)
        else:
            c3 = _fox_prep(xn, mix_g_row, fox_w_t, j, i, 4 * fox_d, fox_bf, tn // hd, B, S)
            gain_spec = pl.BlockSpec((None, 1, hd), lambda jj, ii: (j, 0, 0))
            c3_spec = pl.BlockSpec((None, tm, LANES), lambda jj, ii: (jj, ii, 0))
            qa, w_gate = _mm_norm(
                xn, mix_g_row, fox_w_t, j, i, fox_d, transposed=True, epilogue="fox_q",
                extra=(qn_g, c3), extra_specs=(gain_spec, c3_spec), cfg=(hd, hd ** -0.5 * LOG2E),
                sides=[(ffn_w_gate, i)], tm=tm, tn=tn, name="mm_fox_q")
            ka, w_up = _mm_norm(
                xn, mix_g_row, fox_w_t, j, i, fox_d, col0=fox_d // tn, transposed=True,
                epilogue="fox_k", extra=(kn_g, c3), extra_specs=(gain_spec, c3_spec), cfg=(hd, 1.0),
                sides=[(ffn_w_up, i)], tm=tm, tn=tn, name="mm_fox_k")
            va = _mm_norm(xn, mix_g_row, fox_w_t, j, i, fox_d, col0=2 * fox_d // tn,
                          transposed=True, epilogue="fox_v", cfg=(hd,), tm=tm, tn=tn,
                          name="mm_fox_v")
            gate, w_out = _mm_norm(xn, mix_g_row, fox_w_t, j, i, fox_d, col0=3 * fox_d // tn,
                                   transposed=True, sides=[(fox_w_out, j)], tm=tm, tn=tn,
                                   name="mm_fox_gate")
            y = _fox_attention(qa, ka, va, gate, B, S)
        xf, xn = _mm_residual(y, w_out, xf, gain=(ffn_g_row, i))
        a, w_down = _mm_swiglu(xn, w_gate, w_up, (ffn_w_down, i))
        xf, xn = _mm_residual(a, w_down, xf, last=i == depth - 1)
    return xf.reshape(B, S, D)
```

```python
import functools

import jax
import jax.numpy as jnp
from jax import lax
from jax.experimental import pallas as pl
from jax.experimental.pallas import tpu as pltpu

F32 = jnp.float32
BF16 = jnp.bfloat16

EPS = 1e-6
CHUNK = 128
RET_BLOCK = 256
NORM_ROWS = 32
SWIGLU_ROWS = 1024
ROPE_BASE = 10000.0
RET_HEADS = 8
GMLP_GROUPS = 8
FOX_HEADS = 16
N_MIXERS = 3
LOG2E = 1.4426950408889634

LANES = 128
VMEM_LIMIT = 56 * 1024 * 1024


def _params(n_axes):
    return pltpu.CompilerParams(
        dimension_semantics=("arbitrary",) * n_axes, vmem_limit_bytes=VMEM_LIMIT)


def _gelu_exact(x):
    return 0.5 * x * (1.0 + lax.erf(x * (2.0 ** -0.5)))


def _row_scale(x_ref, ssq_ref):
    ssq = jnp.sum(ssq_ref[...], axis=0)[:, :1]
    return lax.rsqrt(ssq * (1.0 / x_ref.shape[1]) + EPS)


def _side_casts(src_refs, dst_refs):
    for src_ref, dst_ref in zip(src_refs, dst_refs):
        dst_ref[...] = src_ref[...].astype(BF16)


def _side_cast_specs(side, n_j, n_i):
    w_all, layer = side
    K, N = w_all.shape[1:]
    n_steps = n_j * n_i
    rc = next(r for r in range(16, K + 1, 16) if K % r == 0 and K // r <= n_steps)
    last = K // rc - 1
    return (pl.BlockSpec((None, rc, N), lambda j, i: (layer, jnp.minimum(j * n_i + i, last), 0)),
            pl.BlockSpec((rc, N), lambda j, i: (jnp.minimum(j * n_i + i, last), 0)),
            jax.ShapeDtypeStruct((K, N), BF16))


def _mm_norm_kernel(*refs, epilogue, transposed, cfg, n_side):
    x_ref, g_ref, w_ref, ssq_ref = refs[:4]
    wb_ref = refs[-1]
    o_ref = refs[-2 - n_side]
    extra = refs[4:len(refs) - 2 - 2 * n_side]
    _side_casts(refs[len(refs) - 2 - 2 * n_side:len(refs) - 2 - n_side], refs[len(refs) - 1 - n_side:-1])

    @pl.when(pl.program_id(1) == 0)
    def _():
        wb_ref[...] = (w_ref[...] * g_ref[...]).astype(BF16)

    r = _row_scale(x_ref, ssq_ref)
    _mm_norm_slab(slice(0, x_ref.shape[0]), x_ref, wb_ref, r, extra, o_ref,
                  epilogue=epilogue, transposed=transposed, cfg=cfg)


def _mm_norm_slab(rows, x_ref, wb_ref, r, extra, o_ref, *, epilogue, transposed, cfg):
    if transposed:
        acc = lax.dot_general(x_ref[rows, :], wb_ref[...], (((1,), (1,)), ((), ())),
                              preferred_element_type=F32) * r[rows]
    else:
        acc = jnp.dot(x_ref[rows, :], wb_ref[...], preferred_element_type=F32) * r[rows]
    if epilogue == "gelu":
        o_ref[rows, :] = _gelu_exact(acc).astype(o_ref.dtype)
    elif epilogue in ("rope_q", "rope_kt"):
        cos_ref, sin_ref = extra
        dk = cfg[0]
        half = dk // 2
        scale = 1.0 if epilogue == "rope_q" else dk ** -0.5
        cos = cos_ref[rows, :] * scale
        sin = sin_ref[rows, :] * scale
        for h in range(acc.shape[1] // dk):
            t1 = acc[:, h * dk:h * dk + half]
            t2 = acc[:, h * dk + half:(h + 1) * dk]
            r1, r2 = t1 * cos - t2 * sin, t2 * cos + t1 * sin
            if epilogue == "rope_q":
                o_ref[rows, h * dk:h * dk + half] = r1.astype(o_ref.dtype)
                o_ref[rows, h * dk + half:(h + 1) * dk] = r2.astype(o_ref.dtype)
            else:
                o_ref[h * dk:h * dk + half, rows] = r1.T.astype(o_ref.dtype)
                o_ref[h * dk + half:(h + 1) * dk, rows] = r2.T.astype(o_ref.dtype)
    elif epilogue in ("fox_q", "fox_k", "fox_v"):
        hd = cfg[0]
        heads = acc.shape[1] // hd
        lane = lax.broadcasted_iota(jnp.int32, (acc.shape[0], hd), 1)
        if epilogue == "fox_v":
            ones_col = jnp.where(lane == 0, 1.0, 0.0).astype(o_ref.dtype)
        else:
            gain_ref, c3_ref = extra
            gain = gain_ref[...] * cfg[1]
        for h in range(heads):
            t = acc[:, h * hd:(h + 1) * hd]
            if epilogue == "fox_v":
                data, aug = t, ones_col
            else:
                data = t * lax.rsqrt(jnp.mean(t * t, axis=-1, keepdims=True) + EPS) * gain
                hi, mid, lo = (c3_ref[rows, p * heads + h:p * heads + h + 1] for p in range(3))
                if epilogue == "fox_q":
                    aug = jnp.where(lane == 0, hi, jnp.where(lane == 1, mid, jnp.where(
                        lane == 2, lo, jnp.where(lane < 6, 1.0, 0.0))))
                else:
                    aug = jnp.where(lane < 3, 1.0, jnp.where(lane == 3, -hi, jnp.where(
                        lane == 4, -mid, jnp.where(lane == 5, -lo, 0.0))))
            o_ref[rows, 2 * h * hd:(2 * h + 1) * hd] = data.astype(o_ref.dtype)
            o_ref[rows, (2 * h + 1) * hd:(2 * h + 2) * hd] = aug.astype(o_ref.dtype)
    else:
        o_ref[rows, :] = acc.astype(o_ref.dtype)


def _ssq_spec(ssq, tm):
    return pl.BlockSpec((ssq.shape[0], tm, LANES), lambda j, i: (0, i, 0))


def _mm_norm(x, gain, w_all, layer, g_layer, n_out, *, col0=0, transposed=False, epilogue=None,
             extra=(), extra_specs=(), cfg=None, sides=(), out_dtype=BF16, tm=1024, tn=1024,
             name="mm_norm"):
    xb, ssq = x
    T, K = xb.shape
    n_j, n_i = n_out // tn, T // tm
    if transposed:
        g_spec = pl.BlockSpec((None, 1, K), lambda j, i: (g_layer, 0, 0))
        w_spec = pl.BlockSpec((None, tn, K), lambda j, i: (layer, col0 + j, 0))
        wb_shape = (tn, K)
    else:
        g_spec = pl.BlockSpec((None, K, 1), lambda j, i: (g_layer, 0, 0))
        w_spec = pl.BlockSpec((None, K, tn), lambda j, i: (layer, 0, col0 + j))
        wb_shape = (K, tn)
    in_specs = [pl.BlockSpec((tm, K), lambda j, i: (i, 0)), g_spec, w_spec, _ssq_spec(ssq, tm),
                *extra_specs]
    operands = [xb, gain, w_all, ssq, *extra]
    widen = 2 if epilogue in ("fox_q", "fox_k", "fox_v") else 1
    if epilogue == "rope_kt":
        out_specs = [pl.BlockSpec((widen * tn, tm), lambda j, i: (j, i))]
        out_shape = [jax.ShapeDtypeStruct((widen * n_out, T), out_dtype)]
    else:
        out_specs = [pl.BlockSpec((tm, widen * tn), lambda j, i: (i, j))]
        out_shape = [jax.ShapeDtypeStruct((T, widen * n_out), out_dtype)]
    for side in sides:
        src_spec, dst_spec, dst_shape = _side_cast_specs(side, n_j, n_i)
        in_specs.append(src_spec)
        operands.append(side[0])
        out_specs.append(dst_spec)
        out_shape.append(dst_shape)
    out = pl.pallas_call(
        functools.partial(_mm_norm_kernel, epilogue=epilogue, transposed=transposed, cfg=cfg,
                          n_side=len(sides)),
        grid=(n_j, n_i),
        in_specs=in_specs,
        out_specs=out_specs,
        out_shape=out_shape,
        scratch_shapes=[pltpu.VMEM(wb_shape, BF16)],
        compiler_params=_params(2),
        name=name,
    )(*operands)
    return out if sides else out[0]


def _mm_swiglu_kernel(x_ref, wg_ref, wu_ref, ssq_ref, side_src, o_ref, side_dst):
    _side_casts((side_src,), (side_dst,))
    r = _row_scale(x_ref, ssq_ref)
    tm = x_ref.shape[0]
    for r0 in range(0, tm, SWIGLU_ROWS):
        rows = slice(r0, r0 + SWIGLU_ROWS)
        x = x_ref[rows, :]
        gate = jnp.dot(x, wg_ref[...], preferred_element_type=F32) * r[rows]
        up = jnp.dot(x, wu_ref[...], preferred_element_type=F32) * r[rows]
        o_ref[rows, :] = (jax.nn.silu(gate) * up).astype(o_ref.dtype)


def _mm_swiglu(x, wg, wu, side, *, tm=2048, tn=512):
    xb, ssq = x
    T, K = xb.shape
    F = wg.shape[-1]
    n_j, n_i = F // tn, T // tm
    wspec = pl.BlockSpec((K, tn), lambda j, i: (0, j))
    src_spec, dst_spec, dst_shape = _side_cast_specs(side, n_j, n_i)
    return pl.pallas_call(
        _mm_swiglu_kernel,
        grid=(n_j, n_i),
        in_specs=[pl.BlockSpec((tm, K), lambda j, i: (i, 0)), wspec, wspec, _ssq_spec(ssq, tm),
                  src_spec],
        out_specs=[pl.BlockSpec((tm, tn), lambda j, i: (i, j)), dst_spec],
        out_shape=[jax.ShapeDtypeStruct((T, F), BF16), dst_shape],
        compiler_params=_params(2),
        name="mm_swiglu",
    )(xb, wg, wu, ssq, side[0])


def _row_ssq(x):
    return jnp.broadcast_to(jnp.sum(x * x, axis=-1, keepdims=True), (x.shape[0], LANES))


def _entry_kernel(x_ref, pos_ref, invf_ref, xb_ref, ssq_ref, cos_ref, sin_ref):
    x = x_ref[...]
    xb_ref[...] = x.astype(BF16)
    ssq_ref[...] = _row_ssq(x)
    ang = pos_ref[...].astype(F32) * invf_ref[...]
    cos_ref[...] = jnp.cos(ang)
    sin_ref[...] = jnp.sin(ang)


def _entry(x, pos_col, inv_freq, tm=512):
    T, D = x.shape
    half = inv_freq.shape[-1]
    rows = pl.BlockSpec((tm, half), lambda i: (i, 0))
    table = jax.ShapeDtypeStruct((T, half), F32)
    xb, ssq, cos, sin = pl.pallas_call(
        _entry_kernel,
        grid=(T // tm,),
        in_specs=[pl.BlockSpec((tm, D), lambda i: (i, 0)),
                  pl.BlockSpec((tm, 1), lambda i: (i, 0)),
                  pl.BlockSpec((1, half), lambda i: (0, 0))],
        out_specs=[pl.BlockSpec((tm, D), lambda i: (i, 0)),
                   pl.BlockSpec((None, tm, LANES), lambda i: (0, i, 0)), rows, rows],
        out_shape=[jax.ShapeDtypeStruct((T, D), BF16), jax.ShapeDtypeStruct((1, T, LANES), F32),
                   table, table],
        compiler_params=_params(1),
        name="entry",
    )(x, pos_col, inv_freq)
    return (xb, ssq), cos, sin


def _mm_residual_kernel(a_ref, w_ref, r_ref, *refs, mode):
    if mode == "gain":
        gain_ref, refs = refs[0], refs[1:]
    o_ref = refs[0]
    x_new = r_ref[...] + jnp.dot(a_ref[...], w_ref[...], preferred_element_type=F32)
    o_ref[...] = x_new
    if mode != "last":
        ob_ref, ssq_ref = refs[1:]
        ob_ref[...] = (x_new * gain_ref[...] if mode == "gain" else x_new).astype(BF16)
        ssq_ref[...] = _row_ssq(x_new)


def _residual_tiles(K, N):
    return 512, (N if K <= 2048 else N // 2)


def _mm_residual(a, wb, res, *, gain=None, last=False):
    T, K = a.shape
    N = wb.shape[-1]
    tm, tn = _residual_tiles(K, N)
    tile = pl.BlockSpec((tm, tn), lambda j, i: (i, j))
    in_specs = [pl.BlockSpec((tm, K), lambda j, i: (i, 0)),
                pl.BlockSpec((K, tn), lambda j, i: (0, j)),
                tile]
    operands = [a, wb, res]
    mode = "last" if last else ("plain" if gain is None else "gain")
    if mode == "gain":
        g_all, g_layer = gain
        in_specs.append(pl.BlockSpec((None, 1, tn), lambda j, i: (g_layer, 0, j)))
        operands.append(g_all)
    out_specs = [tile, tile, pl.BlockSpec((None, tm, LANES), lambda j, i: (j, i, 0))]
    out_shape = [jax.ShapeDtypeStruct((T, N), F32), jax.ShapeDtypeStruct((T, N), BF16),
                 jax.ShapeDtypeStruct((N // tn, T, LANES), F32)]
    n_out = 1 if last else 3
    out = pl.pallas_call(
        functools.partial(_mm_residual_kernel, mode=mode),
        grid=(N // tn, T // tm),
        in_specs=in_specs,
        out_specs=out_specs[:n_out],
        out_shape=out_shape[:n_out],
        compiler_params=_params(2),
        name="mm_residual",
    )(*operands)
    return out[0], (None if last else (out[1], out[2]))


def _retention_kernel(q_ref, kt_ref, vg_ref, din_ref, qd_ref, kd_ref, cd_ref, gn_ref,
                      y_ref, state_ref, *, heads, dk, dv):
    @pl.when(pl.program_id(1) == 0)
    def _():
        state_ref[...] = jnp.zeros_like(state_ref)

    g_off = heads * dv
    rows = q_ref.shape[0]
    for h in range(heads):
        qb = q_ref[:, h * dk:(h + 1) * dk]
        kt = kt_ref[h * dk:(h + 1) * dk, :]
        v = vg_ref[:, h * dv:(h + 1) * dv]
        scores = jnp.dot(qb, kt, preferred_element_type=F32) * din_ref[h]
        state = state_ref[h]
        q_dec = (qb.astype(F32) * qd_ref[h]).astype(BF16)
        o = (jnp.dot(scores.astype(BF16), v, preferred_element_type=F32)
             + jnp.dot(q_dec, state.astype(BF16), preferred_element_type=F32))
        kt_dec = (kt.astype(F32) * kd_ref[h]).astype(BF16)
        state_ref[h] = state * cd_ref[h] + jnp.dot(kt_dec, v, preferred_element_type=F32)
        gain = gn_ref[:, h * dv:(h + 1) * dv]
        for r0 in range(0, rows, NORM_ROWS):
            strip = o[r0:r0 + NORM_ROWS]
            sc = strip - jnp.mean(strip, axis=-1, keepdims=True)
            var = jnp.mean(sc * sc, axis=-1, keepdims=True)
            g = vg_ref[r0:r0 + NORM_ROWS, g_off + h * dv:g_off + (h + 1) * dv].astype(F32)
            y_ref[r0:r0 + NORM_ROWS, h * dv:(h + 1) * dv] = (
                jax.nn.silu(g) * (sc * lax.rsqrt(var + EPS) * gain)).astype(y_ref.dtype)


def _retention(q, kt, vg, gn_all, layer, batch, seq):
    T = q.shape[0]
    H, C = RET_HEADS, RET_BLOCK
    dk = q.shape[1] // H
    dv = vg.shape[1] // (2 * H)
    nc = seq // C
    log_gamma = jnp.log1p(-jnp.exp2(-5.0 - jnp.arange(H, dtype=F32)))
    idx = jnp.arange(C, dtype=F32)
    dist = idx[:, None] - idx[None, :]
    decay_in = jnp.where(dist >= 0,
                         jnp.exp(jnp.maximum(dist, 0.0)[None] * log_gamma[:, None, None]), 0.0)
    q_decay = jnp.exp((idx + 1.0)[None, :] * log_gamma[:, None])[..., None]
    k_decay = jnp.exp((C - 1.0 - idx)[None, :] * log_gamma[:, None])[:, None, :]
    chunk_decay = jnp.exp(C * log_gamma)[:, None, None]

    def whole(shape):
        return pl.BlockSpec(shape, lambda b, c: (0,) * len(shape))

    return pl.pallas_call(
        functools.partial(_retention_kernel, heads=H, dk=dk, dv=dv),
        grid=(batch, nc),
        in_specs=[pl.BlockSpec((C, H * dk), lambda b, c: (b * nc + c, 0)),
                  pl.BlockSpec((H * dk, C), lambda b, c: (0, b * nc + c)),
                  pl.BlockSpec((C, 2 * H * dv), lambda b, c: (b * nc + c, 0)),
                  whole((H, C, C)), whole((H, C, 1)), whole((H, 1, C)), whole((H, 1, 1)),
                  pl.BlockSpec((None, 1, H * dv), lambda b, c: (layer, 0, 0))],
        out_specs=pl.BlockSpec((C, H * dv), lambda b, c: (b * nc + c, 0)),
        out_shape=jax.ShapeDtypeStruct((T, H * dv), BF16),
        scratch_shapes=[pltpu.VMEM((H, dk, dv), F32)],
        compiler_params=_params(2),
        name="retention",
    )(q, kt, vg, decay_in, q_decay, k_decay, chunk_decay, gn_all)


def _sgu_kernel(u_ref, v_ref, lng_ref, lnb_ref, ws_ref, bs_ref, y_ref, *, groups):
    C = ws_ref.shape[-1]
    gw = v_ref.shape[1] // groups
    row = lax.broadcasted_iota(jnp.int32, (C, C), 0)
    col = lax.broadcasted_iota(jnp.int32, (C, C), 1)
    w_causal = [jnp.where(col <= row, ws_ref[g], 0.0).astype(BF16) for g in range(groups)]
    for c0 in range(0, v_ref.shape[0], C):
        rows = slice(c0, c0 + C)
        v = v_ref[rows, :].astype(F32)
        vc = v - jnp.mean(v, axis=-1, keepdims=True)
        var = jnp.mean(vc * vc, axis=-1, keepdims=True)
        vn = (vc * lax.rsqrt(var + EPS) * lng_ref[...] + lnb_ref[...]).astype(BF16)
        for g in range(groups):
            cols = slice(g * gw, (g + 1) * gw)
            mixed = jnp.dot(w_causal[g], vn[:, cols], preferred_element_type=F32)
            mixed = mixed + bs_ref[:, g:g + 1]
            y_ref[rows, cols] = (u_ref[rows, cols].astype(F32) * mixed).astype(y_ref.dtype)


def _sgu(z, lng_all, lnb_all, ws_all, bs_t, layer):
    T, dffn = z.shape
    half = dffn // 2
    G, C = GMLP_GROUPS, CHUNK
    R = 2 * C
    vec = pl.BlockSpec((None, 1, half), lambda i: (layer, 0, 0))
    return pl.pallas_call(
        functools.partial(_sgu_kernel, groups=G),
        grid=(T // R,),
        in_specs=[pl.BlockSpec((R, half), lambda i: (i, 0)),
                  pl.BlockSpec((R, half), lambda i: (i, 1)),
                  vec, vec,
                  pl.BlockSpec((None, G, C, C), lambda i: (layer, 0, 0, 0)),
                  pl.BlockSpec((None, C, G), lambda i: (layer, 0, 0))],
        out_specs=pl.BlockSpec((R, half), lambda i: (i, 0)),
        out_shape=jax.ShapeDtypeStruct((T, half), BF16),
        compiler_params=_params(1),
        name="sgu",
    )(z, z, lng_all, lnb_all, ws_all, bs_t)


def _fox_prep_kernel(x_ref, g_ref, wf_ref, ssq_ref, bf_ref, c3_ref, wfb_ref, carry_ref, *, group):
    @pl.when(pl.program_id(1) == 0)
    def _():
        carry_ref[...] = jnp.zeros_like(carry_ref)
        wfb_ref[...] = (wf_ref[...] * g_ref[...]).astype(BF16)

    f_logit = lax.dot_general(x_ref[...], wfb_ref[...], (((1,), (1,)), ((), ())),
                              preferred_element_type=F32) * _row_scale(x_ref, ssq_ref)
    log_f = jax.nn.log_sigmoid(f_logit + bf_ref[...])
    tb, heads = log_f.shape
    row = lax.broadcasted_iota(jnp.int32, (tb, tb), 0)
    col = lax.broadcasted_iota(jnp.int32, (tb, tb), 1)
    tri = jnp.where(col <= row, 1.0, 0.0).astype(BF16)

    def split3(x):
        hi = x.astype(BF16)
        r1 = x - hi.astype(F32)
        mid = r1.astype(BF16)
        lo = (r1 - mid.astype(F32)).astype(BF16)
        return hi, mid, lo

    c = carry_ref[...] + sum(jnp.dot(tri, part, preferred_element_type=F32)
                             for part in split3(log_f))
    carry_ref[...] = c[tb - 1:tb, :]
    parts = [part.astype(F32) for part in split3(c * LOG2E)]
    pad = jnp.zeros((tb, LANES - 3 * group), F32)
    for g in range(heads // group):
        c3_ref[g] = jnp.concatenate(
            [part[:, g * group:(g + 1) * group] for part in parts] + [pad], axis=1)


def _fox_prep(x, gain_row, w_t, layer, g_layer, row0, bf_all, group, batch, seq, tb=512):
    xb, ssq = x
    T, K = xb.shape
    H = bf_all.shape[-1]
    nb = seq // tb
    return pl.pallas_call(
        functools.partial(_fox_prep_kernel, group=group),
        grid=(batch, nb),
        in_specs=[pl.BlockSpec((tb, K), lambda b, s: (b * nb + s, 0)),
                  pl.BlockSpec((None, 1, K), lambda b, s: (g_layer, 0, 0)),
                  pl.BlockSpec((None, H, K), lambda b, s: (layer, row0 // H, 0)),
                  pl.BlockSpec((ssq.shape[0], tb, LANES), lambda b, s: (0, b * nb + s, 0)),
                  pl.BlockSpec((None, 1, H), lambda b, s: (layer, 0, 0))],
        out_specs=pl.BlockSpec((H // group, tb, LANES), lambda b, s: (0, b * nb + s, 0)),
        out_shape=jax.ShapeDtypeStruct((H // group, T, LANES), F32),
        scratch_shapes=[pltpu.VMEM((H, K), BF16), pltpu.VMEM((1, H), F32)],
        compiler_params=_params(2),
        name="fox_prep",
    )(xb, gain_row, w_t, ssq, bf_all)


def _fox_attn_kernel(qi_tab, ki_tab, qa_ref, ka_ref, va_ref, g_ref, *refs, heads, hd, n_side):
    y_ref, (m_ref, acc_ref) = refs[n_side], refs[-2:]
    _side_casts(refs[:n_side], refs[n_side + 1:2 * n_side + 1])
    p = pl.program_id(1)
    qi = qi_tab[p]
    ki = ki_tab[p]

    @pl.when(ki == 0)
    def _():
        m_ref[...] = jnp.full_like(m_ref, -jnp.inf)
        acc_ref[...] = jnp.zeros_like(acc_ref)

    tq = qa_ref.shape[0]

    def update(h, r0, r1, n_keys, masked):
        wide = slice(2 * h * hd, 2 * (h + 1) * hd)
        s = lax.dot_general(qa_ref[r0:r1, wide], ka_ref[0:n_keys, wide], (((1,), (1,)), ((), ())),
                            preferred_element_type=F32)
        if masked:
            row = lax.broadcasted_iota(jnp.int32, s.shape, 0) + r0
            col = lax.broadcasted_iota(jnp.int32, s.shape, 1)
            s = jnp.where(col <= row, s, -jnp.inf)
        m_prev = m_ref[h, r0:r1]
        m_new = jnp.maximum(m_prev, jnp.max(s, axis=-1, keepdims=True))
        alpha = jnp.exp2(m_prev - m_new)
        pexp = jnp.exp2(s - m_new[:, :1]).astype(BF16)
        pv = jnp.dot(pexp, va_ref[0:n_keys, wide], preferred_element_type=F32)
        acc_ref[h, r0:r1] = jnp.concatenate([alpha, alpha], axis=1) * acc_ref[h, r0:r1] + pv
        m_ref[h, r0:r1] = m_new

    @pl.when(ki < qi)
    def _():
        for h in range(heads):
            update(h, 0, tq, tq, False)

    @pl.when(ki == qi)
    def _():
        for h in range(heads):
            update(h, 0, tq // 2, tq // 2, True)
            update(h, tq // 2, tq, tq, True)
        for h in range(heads):
            sl = slice(h * hd, (h + 1) * hd)
            acc = acc_ref[h]
            o = acc[:, :hd] / acc[:, hd:hd + 1]
            y_ref[:, sl] = (jax.nn.sigmoid(g_ref[:, sl].astype(F32)) * o).astype(y_ref.dtype)


def _fox_attention(qa, ka, va, gate, batch, seq, sides=(), tq=512):
    T, width = gate.shape
    H = FOX_HEADS
    hd = width // H
    nq = seq // tq
    pairs = [(qi, ki) for qi in range(nq) for ki in range(qi + 1)]
    qi_tab = jnp.asarray([p[0] for p in pairs], jnp.int32)
    ki_tab = jnp.asarray([p[1] for p in pairs], jnp.int32)

    def q_side(cols):
        return pl.BlockSpec((tq, cols), lambda b, p, qt, kt: (b * nq + qt[p], 0))

    def k_side(cols):
        return pl.BlockSpec((tq, cols), lambda b, p, qt, kt: (b * nq + kt[p], 0))

    in_specs = [q_side(2 * width), k_side(2 * width), k_side(2 * width), q_side(width)]
    out_specs = [q_side(width)]
    out_shape = [jax.ShapeDtypeStruct((T, width), BF16)]
    for side in sides:
        src_spec, dst_spec, dst_shape = _side_cast_specs(side, batch, len(pairs))
        in_specs.append(pl.BlockSpec(src_spec.block_shape,
                                     lambda b, p, qt, kt, f=src_spec.index_map: f(b, p)))
        out_specs.append(pl.BlockSpec(dst_spec.block_shape,
                                      lambda b, p, qt, kt, f=dst_spec.index_map: f(b, p)))
        out_shape.append(dst_shape)
    grid_spec = pltpu.PrefetchScalarGridSpec(
        num_scalar_prefetch=2,
        grid=(batch, len(pairs)),
        in_specs=in_specs,
        out_specs=out_specs,
        scratch_shapes=[pltpu.VMEM((H, tq, LANES), F32), pltpu.VMEM((H, tq, 2 * hd), F32)],
    )
    out = pl.pallas_call(
        functools.partial(_fox_attn_kernel, heads=H, hd=hd, n_side=len(sides)),
        grid_spec=grid_spec,
        out_shape=out_shape,
        compiler_params=_params(2),
        name="fox_attention",
    )(qi_tab, ki_tab, qa, ka, va, gate, *(side[0] for side in sides))
    return out if sides else out[0]


def kernel(x, positions, mix_norm_g, ffn_norm_g, ret_w_in, ret_gn_g, ret_w_out, gmlp_w_in, gmlp_ln_g, gmlp_ln_b, gmlp_w_s, gmlp_b_s, gmlp_w_out, fox_w_in, fox_b_f, fox_qn_g, fox_kn_g, fox_w_out, ffn_w_gate, ffn_w_up, ffn_w_down):
    B, S, D = x.shape
    T = B * S
    depth = mix_norm_g.shape[0]
    xf = x.reshape(T, D)

    def rows(a):
        return a.reshape(a.shape[0], 1, a.shape[1])

    mix_g = mix_norm_g[..., None]
    mix_g_row, ffn_g_row = rows(mix_norm_g), rows(ffn_norm_g)
    ret_gn, lng, lnb = rows(ret_gn_g), rows(gmlp_ln_g), rows(gmlp_ln_b)
    qn_g, kn_g, fox_bf = rows(fox_qn_g), rows(fox_kn_g), rows(fox_b_f)
    bs_t = jnp.swapaxes(gmlp_b_s, 1, 2)
    fox_w_t = jnp.swapaxes(fox_w_in, 1, 2)

    tm = tn = 1024
    ret_dk = ret_w_in.shape[-1] // (6 * RET_HEADS)
    ret_qk = 2 * RET_HEADS * ret_dk
    inv_freq = (ROPE_BASE ** (-jnp.arange(ret_dk // 2, dtype=F32) / (ret_dk // 2))).reshape(1, -1)
    xn, cos, sin = _entry(xf, positions.reshape(T, 1), inv_freq)
    rope_spec = pl.BlockSpec((tm, ret_dk // 2), lambda j, i: (i, 0))

    hd = fox_qn_g.shape[-1]
    fox_d = FOX_HEADS * hd
    for i in range(depth):
        kind, j = i % N_MIXERS, i // N_MIXERS
        if kind == 0:
            rope = dict(extra=(cos, sin), extra_specs=(rope_spec, rope_spec), cfg=(ret_dk,),
                        tm=tm, tn=tn)
            q = _mm_norm(xn, mix_g, ret_w_in, j, i, ret_qk // 2, epilogue="rope_q",
                         name="mm_ret_q", **rope)
            kt = _mm_norm(xn, mix_g, ret_w_in, j, i, ret_qk // 2, col0=ret_qk // 2 // tn,
                          epilogue="rope_kt", name="mm_ret_k", **rope)
            vg, w_out, w_gate, w_up = _mm_norm(
                xn, mix_g, ret_w_in, j, i, ret_w_in.shape[-1] - ret_qk, col0=ret_qk // tn,
                sides=[(ret_w_out, j), (ffn_w_gate, i), (ffn_w_up, i)], tm=tm, tn=tn,
                name="mm_ret_vg")
            y = _retention(q, kt, vg, ret_gn, j, B, S)
        elif kind == 1:
            z, w_out, w_gate, w_up = _mm_norm(
                xn, mix_g, gmlp_w_in, j, i, gmlp_w_in.shape[-1], epilogue="gelu",
                sides=[(gmlp_w_out, j), (ffn_w_gate, i), (ffn_w_up, i)], name="mm_gmlp_in")
            y = _sgu(z, lng, lnb, gmlp_w_s, bs_t, j)
        else:
            c3 = _fox_prep(xn, mix_g_row, fox_w_t, j, i, 4 * fox_d, fox_bf, tn // hd, B, S)
            gain_spec = pl.BlockSpec((None, 1, hd), lambda jj, ii: (j, 0, 0))
            c3_spec = pl.BlockSpec((None, tm, LANES), lambda jj, ii: (jj, ii, 0))
            qa = _mm_norm(
                xn, mix_g_row, fox_w_t, j, i, fox_d, transposed=True, epilogue="fox_q",
                extra=(qn_g, c3), extra_specs=(gain_spec, c3_spec), cfg=(hd, hd ** -0.5 * LOG2E),
                tm=tm, tn=tn, name="mm_fox_q")
            ka = _mm_norm(
                xn, mix_g_row, fox_w_t, j, i, fox_d, col0=fox_d // tn, transposed=True,
                epilogue="fox_k", extra=(kn_g, c3), extra_specs=(gain_spec, c3_spec), cfg=(hd, 1.0),
                tm=tm, tn=tn, name="mm_fox_k")
            va = _mm_norm(xn, mix_g_row, fox_w_t, j, i, fox_d, col0=2 * fox_d // tn,
                          transposed=True, epilogue="fox_v", cfg=(hd,), tm=tm, tn=tn,
                          name="mm_fox_v")
            gate, w_out = _mm_norm(xn, mix_g_row, fox_w_t, j, i, fox_d, col0=3 * fox_d // tn,
                                   transposed=True, sides=[(fox_w_out, j)], tm=tm, tn=tn,
                                   name="mm_fox_gate")
            y, w_gate, w_up = _fox_attention(qa, ka, va, gate, B, S,
                                             sides=[(ffn_w_gate, i), (ffn_w_up, i)])
        xf, xn = _mm_residual(y, w_out, xf, gain=(ffn_g_row, i))
        a, w_down = _mm_swiglu(xn, w_gate, w_up, (ffn_w_down, i))
        xf, xn = _mm_residual(a, w_down, xf, last=i == depth - 1)
    return xf.reshape(B, S, D)
```

```python
import functools

import jax
import jax.numpy as jnp
from jax import lax
from jax.experimental import pallas as pl
from jax.experimental.pallas import tpu as pltpu

F32 = jnp.float32
BF16 = jnp.bfloat16

EPS = 1e-6
CHUNK = 128
RET_BLOCK = 256
NORM_ROWS = 32
SWIGLU_ROWS = 1024
ROPE_BASE = 10000.0
RET_HEADS = 8
GMLP_GROUPS = 8
FOX_HEADS = 16
N_MIXERS = 3
LOG2E = 1.4426950408889634

LANES = 128
VMEM_LIMIT = 56 * 1024 * 1024


def _params(n_axes):
    return pltpu.CompilerParams(
        dimension_semantics=("arbitrary",) * n_axes, vmem_limit_bytes=VMEM_LIMIT)


def _gelu_exact(x):
    return 0.5 * x * (1.0 + lax.erf(x * (2.0 ** -0.5)))


def _row_scale(x_ref, ssq_ref):
    ssq = jnp.sum(ssq_ref[...], axis=0)[:, :1]
    return lax.rsqrt(ssq * (1.0 / x_ref.shape[1]) + EPS)


def _side_casts(src_refs, dst_refs):
    for src_ref, dst_ref in zip(src_refs, dst_refs):
        dst_ref[...] = src_ref[...].astype(BF16)


def _side_cast_specs(side, n_j, n_i):
    w_all, layer = side
    K, N = w_all.shape[1:]
    n_steps = n_j * n_i
    rc = next(r for r in range(16, K + 1, 16) if K % r == 0 and K // r <= n_steps)
    last = K // rc - 1
    return (pl.BlockSpec((None, rc, N), lambda j, i: (layer, jnp.minimum(j * n_i + i, last), 0)),
            pl.BlockSpec((rc, N), lambda j, i: (jnp.minimum(j * n_i + i, last), 0)),
            jax.ShapeDtypeStruct((K, N), BF16))


def _mm_norm_kernel(*refs, epilogue, transposed, cfg, n_side):
    x_ref, g_ref, w_ref, ssq_ref = refs[:4]
    wb_ref = refs[-1]
    o_ref = refs[-2 - n_side]
    extra = refs[4:len(refs) - 2 - 2 * n_side]
    _side_casts(refs[len(refs) - 2 - 2 * n_side:len(refs) - 2 - n_side], refs[len(refs) - 1 - n_side:-1])

    @pl.when(pl.program_id(1) == 0)
    def _():
        wb_ref[...] = (w_ref[...] * g_ref[...]).astype(BF16)

    r = _row_scale(x_ref, ssq_ref)
    _mm_norm_slab(slice(0, x_ref.shape[0]), x_ref, wb_ref, r, extra, o_ref,
                  epilogue=epilogue, transposed=transposed, cfg=cfg)


def _mm_norm_slab(rows, x_ref, wb_ref, r, extra, o_ref, *, epilogue, transposed, cfg):
    if transposed:
        acc = lax.dot_general(x_ref[rows, :], wb_ref[...], (((1,), (1,)), ((), ())),
                              preferred_element_type=F32) * r[rows]
    else:
        acc = jnp.dot(x_ref[rows, :], wb_ref[...], preferred_element_type=F32) * r[rows]
    if epilogue == "gelu":
        o_ref[rows, :] = _gelu_exact(acc).astype(o_ref.dtype)
    elif epilogue in ("rope_q", "rope_kt"):
        cos_ref, sin_ref = extra
        dk = cfg[0]
        half = dk // 2
        scale = 1.0 if epilogue == "rope_q" else dk ** -0.5
        cos = cos_ref[rows, :] * scale
        sin = sin_ref[rows, :] * scale
        for h in range(acc.shape[1] // dk):
            t1 = acc[:, h * dk:h * dk + half]
            t2 = acc[:, h * dk + half:(h + 1) * dk]
            r1, r2 = t1 * cos - t2 * sin, t2 * cos + t1 * sin
            if epilogue == "rope_q":
                o_ref[rows, h * dk:h * dk + half] = r1.astype(o_ref.dtype)
                o_ref[rows, h * dk + half:(h + 1) * dk] = r2.astype(o_ref.dtype)
            else:
                o_ref[h * dk:h * dk + half, rows] = r1.T.astype(o_ref.dtype)
                o_ref[h * dk + half:(h + 1) * dk, rows] = r2.T.astype(o_ref.dtype)
    elif epilogue in ("fox_q", "fox_k", "fox_v"):
        hd = cfg[0]
        heads = acc.shape[1] // hd
        lane = lax.broadcasted_iota(jnp.int32, (acc.shape[0], hd), 1)
        if epilogue == "fox_v":
            ones_col = jnp.where(lane == 0, 1.0, 0.0).astype(o_ref.dtype)
        else:
            gain_ref, c3_ref = extra
            gain = gain_ref[...] * cfg[1]
        for h in range(heads):
            t = acc[:, h * hd:(h + 1) * hd]
            if epilogue == "fox_v":
                data, aug = t, ones_col
            else:
                data = t * lax.rsqrt(jnp.mean(t * t, axis=-1, keepdims=True) + EPS) * gain
                hi, mid, lo = (c3_ref[rows, p * heads + h:p * heads + h + 1] for p in range(3))
                if epilogue == "fox_q":
                    aug = jnp.where(lane == 0, hi, jnp.where(lane == 1, mid, jnp.where(
                        lane == 2, lo, jnp.where(lane < 6, 1.0, 0.0))))
                else:
                    aug = jnp.where(lane < 3, 1.0, jnp.where(lane == 3, -hi, jnp.where(
                        lane == 4, -mid, jnp.where(lane == 5, -lo, 0.0))))
            o_ref[rows, 2 * h * hd:(2 * h + 1) * hd] = data.astype(o_ref.dtype)
            o_ref[rows, (2 * h + 1) * hd:(2 * h + 2) * hd] = aug.astype(o_ref.dtype)
    else:
        o_ref[rows, :] = acc.astype(o_ref.dtype)


def _ssq_spec(ssq, tm):
    return pl.BlockSpec((ssq.shape[0], tm, LANES), lambda j, i: (0, i, 0))


def _mm_norm(x, gain, w_all, layer, g_layer, n_out, *, col0=0, transposed=False, epilogue=None,
             extra=(), extra_specs=(), cfg=None, sides=(), out_dtype=BF16, tm=1024, tn=1024,
             name="mm_norm"):
    xb, ssq = x
    T, K = xb.shape
    n_j, n_i = n_out // tn, T // tm
    if transposed:
        g_spec = pl.BlockSpec((None, 1, K), lambda j, i: (g_layer, 0, 0))
        w_spec = pl.BlockSpec((None, tn, K), lambda j, i: (layer, col0 + j, 0))
        wb_shape = (tn, K)
    else:
        g_spec = pl.BlockSpec((None, K, 1), lambda j, i: (g_layer, 0, 0))
        w_spec = pl.BlockSpec((None, K, tn), lambda j, i: (layer, 0, col0 + j))
        wb_shape = (K, tn)
    in_specs = [pl.BlockSpec((tm, K), lambda j, i: (i, 0)), g_spec, w_spec, _ssq_spec(ssq, tm),
                *extra_specs]
    operands = [xb, gain, w_all, ssq, *extra]
    widen = 2 if epilogue in ("fox_q", "fox_k", "fox_v") else 1
    if epilogue == "rope_kt":
        out_specs = [pl.BlockSpec((widen * tn, tm), lambda j, i: (j, i))]
        out_shape = [jax.ShapeDtypeStruct((widen * n_out, T), out_dtype)]
    else:
        out_specs = [pl.BlockSpec((tm, widen * tn), lambda j, i: (i, j))]
        out_shape = [jax.ShapeDtypeStruct((T, widen * n_out), out_dtype)]
    for side in sides:
        src_spec, dst_spec, dst_shape = _side_cast_specs(side, n_j, n_i)
        in_specs.append(src_spec)
        operands.append(side[0])
        out_specs.append(dst_spec)
        out_shape.append(dst_shape)
    out = pl.pallas_call(
        functools.partial(_mm_norm_kernel, epilogue=epilogue, transposed=transposed, cfg=cfg,
                          n_side=len(sides)),
        grid=(n_j, n_i),
        in_specs=in_specs,
        out_specs=out_specs,
        out_shape=out_shape,
        scratch_shapes=[pltpu.VMEM(wb_shape, BF16)],
        compiler_params=_params(2),
        name=name,
    )(*operands)
    return out if sides else out[0]


def _mm_swiglu_kernel(x_ref, wg_ref, wu_ref, ssq_ref, side_src, o_ref, side_dst):
    _side_casts((side_src,), (side_dst,))
    r = _row_scale(x_ref, ssq_ref)
    tm = x_ref.shape[0]
    for r0 in range(0, tm, SWIGLU_ROWS):
        rows = slice(r0, r0 + SWIGLU_ROWS)
        x = x_ref[rows, :]
        gate = jnp.dot(x, wg_ref[...], preferred_element_type=F32) * r[rows]
        up = jnp.dot(x, wu_ref[...], preferred_element_type=F32) * r[rows]
        o_ref[rows, :] = (jax.nn.silu(gate) * up).astype(o_ref.dtype)


def _mm_swiglu(x, wg, wu, side, *, tm=2048, tn=512):
    xb, ssq = x
    T, K = xb.shape
    F = wg.shape[-1]
    n_j, n_i = F // tn, T // tm
    wspec = pl.BlockSpec((K, tn), lambda j, i: (0, j))
    src_spec, dst_spec, dst_shape = _side_cast_specs(side, n_j, n_i)
    return pl.pallas_call(
        _mm_swiglu_kernel,
        grid=(n_j, n_i),
        in_specs=[pl.BlockSpec((tm, K), lambda j, i: (i, 0)), wspec, wspec, _ssq_spec(ssq, tm),
                  src_spec],
        out_specs=[pl.BlockSpec((tm, tn), lambda j, i: (i, j)), dst_spec],
        out_shape=[jax.ShapeDtypeStruct((T, F), BF16), dst_shape],
        compiler_params=_params(2),
        name="mm_swiglu",
    )(xb, wg, wu, ssq, side[0])


def _row_ssq(x):
    return jnp.broadcast_to(jnp.sum(x * x, axis=-1, keepdims=True), (x.shape[0], LANES))


def _entry_kernel(x_ref, pos_ref, invf_ref, xb_ref, ssq_ref, cos_ref, sin_ref):
    x = x_ref[...]
    xb_ref[...] = x.astype(BF16)
    ssq_ref[...] = _row_ssq(x)
    ang = pos_ref[...].astype(F32) * invf_ref[...]
    cos_ref[...] = jnp.cos(ang)
    sin_ref[...] = jnp.sin(ang)


def _entry(x, pos_col, inv_freq, tm=512):
    T, D = x.shape
    half = inv_freq.shape[-1]
    rows = pl.BlockSpec((tm, half), lambda i: (i, 0))
    table = jax.ShapeDtypeStruct((T, half), F32)
    xb, ssq, cos, sin = pl.pallas_call(
        _entry_kernel,
        grid=(T // tm,),
        in_specs=[pl.BlockSpec((tm, D), lambda i: (i, 0)),
                  pl.BlockSpec((tm, 1), lambda i: (i, 0)),
                  pl.BlockSpec((1, half), lambda i: (0, 0))],
        out_specs=[pl.BlockSpec((tm, D), lambda i: (i, 0)),
                   pl.BlockSpec((None, tm, LANES), lambda i: (0, i, 0)), rows, rows],
        out_shape=[jax.ShapeDtypeStruct((T, D), BF16), jax.ShapeDtypeStruct((1, T, LANES), F32),
                   table, table],
        compiler_params=_params(1),
        name="entry",
    )(x, pos_col, inv_freq)
    return (xb, ssq), cos, sin


def _mm_residual_kernel(a_ref, w_ref, r_ref, *refs, mode):
    if mode == "gain":
        gain_ref, refs = refs[0], refs[1:]
    o_ref = refs[0]
    x_new = r_ref[...] + jnp.dot(a_ref[...], w_ref[...], preferred_element_type=F32)
    o_ref[...] = x_new
    if mode != "last":
        ob_ref, ssq_ref = refs[1:]
        ob_ref[...] = (x_new * gain_ref[...] if mode == "gain" else x_new).astype(BF16)
        ssq_ref[...] = _row_ssq(x_new)


def _residual_tiles(K, N):
    return 512, (N if K <= 2048 else N // 2)


def _mm_residual(a, wb, res, *, gain=None, last=False):
    T, K = a.shape
    N = wb.shape[-1]
    tm, tn = _residual_tiles(K, N)
    tile = pl.BlockSpec((tm, tn), lambda j, i: (i, j))
    in_specs = [pl.BlockSpec((tm, K), lambda j, i: (i, 0)),
                pl.BlockSpec((K, tn), lambda j, i: (0, j)),
                tile]
    operands = [a, wb, res]
    mode = "last" if last else ("plain" if gain is None else "gain")
    if mode == "gain":
        g_all, g_layer = gain
        in_specs.append(pl.BlockSpec((None, 1, tn), lambda j, i: (g_layer, 0, j)))
        operands.append(g_all)
    out_specs = [tile, tile, pl.BlockSpec((None, tm, LANES), lambda j, i: (j, i, 0))]
    out_shape = [jax.ShapeDtypeStruct((T, N), F32), jax.ShapeDtypeStruct((T, N), BF16),
                 jax.ShapeDtypeStruct((N // tn, T, LANES), F32)]
    n_out = 1 if last else 3
    out = pl.pallas_call(
        functools.partial(_mm_residual_kernel, mode=mode),
        grid=(N // tn, T // tm),
        in_specs=in_specs,
        out_specs=out_specs[:n_out],
        out_shape=out_shape[:n_out],
        compiler_params=_params(2),
        name="mm_residual",
    )(*operands)
    return out[0], (None if last else (out[1], out[2]))


def _retention_kernel(q_ref, kt_ref, vg_ref, din_ref, qd_ref, kd_ref, cd_ref, gn_ref,
                      y_ref, state_ref, *, heads, dk, dv):
    @pl.when(pl.program_id(1) == 0)
    def _():
        state_ref[...] = jnp.zeros_like(state_ref)

    g_off = heads * dv
    rows = q_ref.shape[0]
    for h in range(heads):
        qb = q_ref[:, h * dk:(h + 1) * dk]
        kt = kt_ref[h * dk:(h + 1) * dk, :]
        v = vg_ref[:, h * dv:(h + 1) * dv]
        scores = jnp.dot(qb, kt, preferred_element_type=F32) * din_ref[h]
        state = state_ref[h]
        q_dec = qb * qd_ref[h]
        o = (jnp.dot(scores.astype(BF16), v, preferred_element_type=F32)
             + jnp.dot(q_dec, state.astype(BF16), preferred_element_type=F32))
        kt_dec = kt * kd_ref[h]
        state_ref[h] = state * cd_ref[h] + jnp.dot(kt_dec, v, preferred_element_type=F32)
        gain = gn_ref[:, h * dv:(h + 1) * dv]
        for r0 in range(0, rows, NORM_ROWS):
            strip = o[r0:r0 + NORM_ROWS]
            sc = strip - jnp.mean(strip, axis=-1, keepdims=True)
            var = jnp.mean(sc * sc, axis=-1, keepdims=True)
            g = vg_ref[r0:r0 + NORM_ROWS, g_off + h * dv:g_off + (h + 1) * dv].astype(F32)
            y_ref[r0:r0 + NORM_ROWS, h * dv:(h + 1) * dv] = (
                jax.nn.silu(g) * (sc * lax.rsqrt(var + EPS) * gain)).astype(y_ref.dtype)


def _retention(q, kt, vg, gn_all, layer, batch, seq):
    T = q.shape[0]
    H, C = RET_HEADS, RET_BLOCK
    dk = q.shape[1] // H
    dv = vg.shape[1] // (2 * H)
    nc = seq // C
    log_gamma = jnp.log1p(-jnp.exp2(-5.0 - jnp.arange(H, dtype=F32)))
    idx = jnp.arange(C, dtype=F32)
    dist = idx[:, None] - idx[None, :]
    decay_in = jnp.where(dist >= 0,
                         jnp.exp(jnp.maximum(dist, 0.0)[None] * log_gamma[:, None, None]), 0.0)
    q_decay = jnp.exp((idx + 1.0)[None, :] * log_gamma[:, None])[..., None]
    q_decay = jnp.broadcast_to(q_decay, (H, C, dk)).astype(BF16)
    k_decay = jnp.exp((C - 1.0 - idx)[None, :] * log_gamma[:, None])[:, None, :]
    k_decay = jnp.broadcast_to(k_decay, (H, dk, C)).astype(BF16)
    chunk_decay = jnp.exp(C * log_gamma)[:, None, None]

    def whole(shape):
        return pl.BlockSpec(shape, lambda b, c: (0,) * len(shape))

    return pl.pallas_call(
        functools.partial(_retention_kernel, heads=H, dk=dk, dv=dv),
        grid=(batch, nc),
        in_specs=[pl.BlockSpec((C, H * dk), lambda b, c: (b * nc + c, 0)),
                  pl.BlockSpec((H * dk, C), lambda b, c: (0, b * nc + c)),
                  pl.BlockSpec((C, 2 * H * dv), lambda b, c: (b * nc + c, 0)),
                  whole((H, C, C)), whole((H, C, dk)), whole((H, dk, C)), whole((H, 1, 1)),
                  pl.BlockSpec((None, 1, H * dv), lambda b, c: (layer, 0, 0))],
        out_specs=pl.BlockSpec((C, H * dv), lambda b, c: (b * nc + c, 0)),
        out_shape=jax.ShapeDtypeStruct((T, H * dv), BF16),
        scratch_shapes=[pltpu.VMEM((H, dk, dv), F32)],
        compiler_params=_params(2),
        name="retention",
    )(q, kt, vg, decay_in, q_decay, k_decay, chunk_decay, gn_all)


def _sgu_kernel(u_ref, v_ref, lng_ref, lnb_ref, ws_ref, bs_ref, y_ref, *, groups):
    C = ws_ref.shape[-1]
    gw = v_ref.shape[1] // groups
    row = lax.broadcasted_iota(jnp.int32, (C, C), 0)
    col = lax.broadcasted_iota(jnp.int32, (C, C), 1)
    w_causal = [jnp.where(col <= row, ws_ref[g], 0.0).astype(BF16) for g in range(groups)]
    for c0 in range(0, v_ref.shape[0], C):
        rows = slice(c0, c0 + C)
        v = v_ref[rows, :].astype(F32)
        vc = v - jnp.mean(v, axis=-1, keepdims=True)
        var = jnp.mean(vc * vc, axis=-1, keepdims=True)
        vn = (vc * lax.rsqrt(var + EPS) * lng_ref[...] + lnb_ref[...]).astype(BF16)
        for g in range(groups):
            cols = slice(g * gw, (g + 1) * gw)
            mixed = jnp.dot(w_causal[g], vn[:, cols], preferred_element_type=F32)
            mixed = mixed + bs_ref[:, g:g + 1]
            y_ref[rows, cols] = u_ref[rows, cols] * mixed.astype(BF16)


def _sgu(z, lng_all, lnb_all, ws_all, bs_t, layer):
    T, dffn = z.shape
    half = dffn // 2
    G, C = GMLP_GROUPS, CHUNK
    R = 2 * C
    vec = pl.BlockSpec((None, 1, half), lambda i: (layer, 0, 0))
    return pl.pallas_call(
        functools.partial(_sgu_kernel, groups=G),
        grid=(T // R,),
        in_specs=[pl.BlockSpec((R, half), lambda i: (i, 0)),
                  pl.BlockSpec((R, half), lambda i: (i, 1)),
                  vec, vec,
                  pl.BlockSpec((None, G, C, C), lambda i: (layer, 0, 0, 0)),
                  pl.BlockSpec((None, C, G), lambda i: (layer, 0, 0))],
        out_specs=pl.BlockSpec((R, half), lambda i: (i, 0)),
        out_shape=jax.ShapeDtypeStruct((T, half), BF16),
        compiler_params=_params(1),
        name="sgu",
    )(z, z, lng_all, lnb_all, ws_all, bs_t)


def _fox_prep_kernel(x_ref, g_ref, wf_ref, ssq_ref, bf_ref, c3_ref, wfb_ref, carry_ref, *, group):
    @pl.when(pl.program_id(1) == 0)
    def _():
        carry_ref[...] = jnp.zeros_like(carry_ref)
        wfb_ref[...] = (wf_ref[...] * g_ref[...]).astype(BF16)

    f_logit = lax.dot_general(x_ref[...], wfb_ref[...], (((1,), (1,)), ((), ())),
                              preferred_element_type=F32) * _row_scale(x_ref, ssq_ref)
    log_f = jax.nn.log_sigmoid(f_logit + bf_ref[...])
    tb, heads = log_f.shape
    row = lax.broadcasted_iota(jnp.int32, (tb, tb), 0)
    col = lax.broadcasted_iota(jnp.int32, (tb, tb), 1)
    tri = jnp.where(col <= row, 1.0, 0.0).astype(BF16)

    def split3(x):
        hi = x.astype(BF16)
        r1 = x - hi.astype(F32)
        mid = r1.astype(BF16)
        lo = (r1 - mid.astype(F32)).astype(BF16)
        return hi, mid, lo

    c = carry_ref[...] + sum(jnp.dot(tri, part, preferred_element_type=F32)
                             for part in split3(log_f))
    carry_ref[...] = c[tb - 1:tb, :]
    parts = [part.astype(F32) for part in split3(c * LOG2E)]
    pad = jnp.zeros((tb, LANES - 3 * group), F32)
    for g in range(heads // group):
        c3_ref[g] = jnp.concatenate(
            [part[:, g * group:(g + 1) * group] for part in parts] + [pad], axis=1)


def _fox_prep(x, gain_row, w_t, layer, g_layer, row0, bf_all, group, batch, seq, tb=512):
    xb, ssq = x
    T, K = xb.shape
    H = bf_all.shape[-1]
    nb = seq // tb
    return pl.pallas_call(
        functools.partial(_fox_prep_kernel, group=group),
        grid=(batch, nb),
        in_specs=[pl.BlockSpec((tb, K), lambda b, s: (b * nb + s, 0)),
                  pl.BlockSpec((None, 1, K), lambda b, s: (g_layer, 0, 0)),
                  pl.BlockSpec((None, H, K), lambda b, s: (layer, row0 // H, 0)),
                  pl.BlockSpec((ssq.shape[0], tb, LANES), lambda b, s: (0, b * nb + s, 0)),
                  pl.BlockSpec((None, 1, H), lambda b, s: (layer, 0, 0))],
        out_specs=pl.BlockSpec((H // group, tb, LANES), lambda b, s: (0, b * nb + s, 0)),
        out_shape=jax.ShapeDtypeStruct((H // group, T, LANES), F32),
        scratch_shapes=[pltpu.VMEM((H, K), BF16), pltpu.VMEM((1, H), F32)],
        compiler_params=_params(2),
        name="fox_prep",
    )(xb, gain_row, w_t, ssq, bf_all)


def _fox_attn_kernel(qi_tab, ki_tab, qa_ref, ka_ref, va_ref, g_ref, *refs, heads, hd, n_side):
    y_ref, (m_ref, acc_ref) = refs[n_side], refs[-2:]
    _side_casts(refs[:n_side], refs[n_side + 1:2 * n_side + 1])
    p = pl.program_id(1)
    qi = qi_tab[p]
    ki = ki_tab[p]

    @pl.when(ki == 0)
    def _():
        m_ref[...] = jnp.full_like(m_ref, -jnp.inf)
        acc_ref[...] = jnp.zeros_like(acc_ref)

    tq = qa_ref.shape[0]

    def update(h, r0, r1, n_keys, masked):
        wide = slice(2 * h * hd, 2 * (h + 1) * hd)
        s = lax.dot_general(qa_ref[r0:r1, wide], ka_ref[0:n_keys, wide], (((1,), (1,)), ((), ())),
                            preferred_element_type=F32)
        if masked:
            row = lax.broadcasted_iota(jnp.int32, s.shape, 0) + r0
            col = lax.broadcasted_iota(jnp.int32, s.shape, 1)
            s = jnp.where(col <= row, s, -jnp.inf)
        m_prev = m_ref[h, r0:r1]
        m_new = jnp.maximum(m_prev, jnp.max(s, axis=-1, keepdims=True))
        alpha = jnp.exp2(m_prev - m_new)
        pexp = jnp.exp2(s - m_new[:, :1]).astype(BF16)
        pv = jnp.dot(pexp, va_ref[0:n_keys, wide], preferred_element_type=F32)
        acc_ref[h, r0:r1] = jnp.concatenate([alpha, alpha], axis=1) * acc_ref[h, r0:r1] + pv
        m_ref[h, r0:r1] = m_new

    @pl.when(ki < qi)
    def _():
        for h in range(heads):
            update(h, 0, tq, tq, False)

    @pl.when(ki == qi)
    def _():
        for h in range(heads):
            update(h, 0, tq // 2, tq // 2, True)
            update(h, tq // 2, tq, tq, True)
        for h in range(heads):
            sl = slice(h * hd, (h + 1) * hd)
            acc = acc_ref[h]
            o = acc[:, :hd] / acc[:, hd:hd + 1]
            y_ref[:, sl] = (jax.nn.sigmoid(g_ref[:, sl].astype(F32)) * o).astype(y_ref.dtype)


def _fox_attention(qa, ka, va, gate, batch, seq, sides=(), tq=512):
    T, width = gate.shape
    H = FOX_HEADS
    hd = width // H
    nq = seq // tq
    pairs = [(qi, ki) for qi in range(nq) for ki in range(qi + 1)]
    qi_tab = jnp.asarray([p[0] for p in pairs], jnp.int32)
    ki_tab = jnp.asarray([p[1] for p in pairs], jnp.int32)

    def q_side(cols):
        return pl.BlockSpec((tq, cols), lambda b, p, qt, kt: (b * nq + qt[p], 0))

    def k_side(cols):
        return pl.BlockSpec((tq, cols), lambda b, p, qt, kt: (b * nq + kt[p], 0))

    in_specs = [q_side(2 * width), k_side(2 * width), k_side(2 * width), q_side(width)]
    out_specs = [q_side(width)]
    out_shape = [jax.ShapeDtypeStruct((T, width), BF16)]
    for side in sides:
        src_spec, dst_spec, dst_shape = _side_cast_specs(side, batch, len(pairs))
        in_specs.append(pl.BlockSpec(src_spec.block_shape,
                                     lambda b, p, qt, kt, f=src_spec.index_map: f(b, p)))
        out_specs.append(pl.BlockSpec(dst_spec.block_shape,
                                      lambda b, p, qt, kt, f=dst_spec.index_map: f(b, p)))
        out_shape.append(dst_shape)
    grid_spec = pltpu.PrefetchScalarGridSpec(
        num_scalar_prefetch=2,
        grid=(batch, len(pairs)),
        in_specs=in_specs,
        out_specs=out_specs,
        scratch_shapes=[pltpu.VMEM((H, tq, LANES), F32), pltpu.VMEM((H, tq, 2 * hd), F32)],
    )
    out = pl.pallas_call(
        functools.partial(_fox_attn_kernel, heads=H, hd=hd, n_side=len(sides)),
        grid_spec=grid_spec,
        out_shape=out_shape,
        compiler_params=_params(2),
        name="fox_attention",
    )(qi_tab, ki_tab, qa, ka, va, gate, *(side[0] for side in sides))
    return out if sides else out[0]


def kernel(x, positions, mix_norm_g, ffn_norm_g, ret_w_in, ret_gn_g, ret_w_out, gmlp_w_in, gmlp_ln_g, gmlp_ln_b, gmlp_w_s, gmlp_b_s, gmlp_w_out, fox_w_in, fox_b_f, fox_qn_g, fox_kn_g, fox_w_out, ffn_w_gate, ffn_w_up, ffn_w_down):
    B, S, D = x.shape
    T = B * S
    depth = mix_norm_g.shape[0]
    xf = x.reshape(T, D)

    def rows(a):
        return a.reshape(a.shape[0], 1, a.shape[1])

    mix_g = mix_norm_g[..., None]
    mix_g_row, ffn_g_row = rows(mix_norm_g), rows(ffn_norm_g)
    ret_gn, lng, lnb = rows(ret_gn_g), rows(gmlp_ln_g), rows(gmlp_ln_b)
    qn_g, kn_g, fox_bf = rows(fox_qn_g), rows(fox_kn_g), rows(fox_b_f)
    bs_t = jnp.swapaxes(gmlp_b_s, 1, 2)
    fox_w_t = jnp.swapaxes(fox_w_in, 1, 2)

    tm = tn = 1024
    ret_dk = ret_w_in.shape[-1] // (6 * RET_HEADS)
    ret_qk = 2 * RET_HEADS * ret_dk
    inv_freq = (ROPE_BASE ** (-jnp.arange(ret_dk // 2, dtype=F32) / (ret_dk // 2))).reshape(1, -1)
    xn, cos, sin = _entry(xf, positions.reshape(T, 1), inv_freq)
    rope_spec = pl.BlockSpec((tm, ret_dk // 2), lambda j, i: (i, 0))

    hd = fox_qn_g.shape[-1]
    fox_d = FOX_HEADS * hd
    for i in range(depth):
        kind, j = i % N_MIXERS, i // N_MIXERS
        if kind == 0:
            rope = dict(extra=(cos, sin), extra_specs=(rope_spec, rope_spec), cfg=(ret_dk,),
                        tm=tm, tn=tn)
            q = _mm_norm(xn, mix_g, ret_w_in, j, i, ret_qk // 2, epilogue="rope_q",
                         name="mm_ret_q", **rope)
            kt = _mm_norm(xn, mix_g, ret_w_in, j, i, ret_qk // 2, col0=ret_qk // 2 // tn,
                          epilogue="rope_kt", name="mm_ret_k", **rope)
            vg, w_out, w_gate, w_up = _mm_norm(
                xn, mix_g, ret_w_in, j, i, ret_w_in.shape[-1] - ret_qk, col0=ret_qk // tn,
                sides=[(ret_w_out, j), (ffn_w_gate, i), (ffn_w_up, i)], tm=tm, tn=tn,
                name="mm_ret_vg")
            y = _retention(q, kt, vg, ret_gn, j, B, S)
        elif kind == 1:
            z, w_out, w_gate, w_up = _mm_norm(
                xn, mix_g, gmlp_w_in, j, i, gmlp_w_in.shape[-1], epilogue="gelu",
                sides=[(gmlp_w_out, j), (ffn_w_gate, i), (ffn_w_up, i)], name="mm_gmlp_in")
            y = _sgu(z, lng, lnb, gmlp_w_s, bs_t, j)
        else:
            c3 = _fox_prep(xn, mix_g_row, fox_w_t, j, i, 4 * fox_d, fox_bf, tn // hd, B, S)
            gain_spec = pl.BlockSpec((None, 1, hd), lambda jj, ii: (j, 0, 0))
            c3_spec = pl.BlockSpec((None, tm, LANES), lambda jj, ii: (jj, ii, 0))
            qa = _mm_norm(
                xn, mix_g_row, fox_w_t, j, i, fox_d, transposed=True, epilogue="fox_q",
                extra=(qn_g, c3), extra_specs=(gain_spec, c3_spec), cfg=(hd, hd ** -0.5 * LOG2E),
                tm=tm, tn=tn, name="mm_fox_q")
            ka = _mm_norm(
                xn, mix_g_row, fox_w_t, j, i, fox_d, col0=fox_d // tn, transposed=True,
                epilogue="fox_k", extra=(kn_g, c3), extra_specs=(gain_spec, c3_spec), cfg=(hd, 1.0),
                tm=tm, tn=tn, name="mm_fox_k")
            va = _mm_norm(xn, mix_g_row, fox_w_t, j, i, fox_d, col0=2 * fox_d // tn,
                          transposed=True, epilogue="fox_v", cfg=(hd,), tm=tm, tn=tn,
                          name="mm_fox_v")
            gate, w_out = _mm_norm(xn, mix_g_row, fox_w_t, j, i, fox_d, col0=3 * fox_d // tn,
                                   transposed=True, sides=[(fox_w_out, j)], tm=tm, tn=tn,
                                   name="mm_fox_gate")
            y, w_gate, w_up = _fox_attention(qa, ka, va, gate, B, S,
                                             sides=[(ffn_w_gate, i), (ffn_w_up, i)])
        xf, xn = _mm_residual(y, w_out, xf, gain=(ffn_g_row, i))
        a, w_down = _mm_swiglu(xn, w_gate, w_up, (ffn_w_down, i))
        xf, xn = _mm_residual(a, w_down, xf, last=i == depth - 1)
    return xf.reshape(B, S, D)
```

```python
import functools

import jax
import jax.numpy as jnp
from jax import lax
from jax.experimental import pallas as pl
from jax.experimental.pallas import tpu as pltpu

F32 = jnp.float32
BF16 = jnp.bfloat16

EPS = 1e-6
CHUNK = 128
RET_BLOCK = 256
NORM_ROWS = 32
SWIGLU_ROWS = 1024
ROPE_BASE = 10000.0
RET_HEADS = 8
GMLP_GROUPS = 8
FOX_HEADS = 16
N_MIXERS = 3
LOG2E = 1.4426950408889634

LANES = 128
VMEM_LIMIT = 56 * 1024 * 1024


def _params(n_axes):
    return pltpu.CompilerParams(
        dimension_semantics=("arbitrary",) * n_axes, vmem_limit_bytes=VMEM_LIMIT)


def _gelu_exact(x):
    return 0.5 * x * (1.0 + lax.erf(x * (2.0 ** -0.5)))


def _row_scale(x_ref, ssq_ref):
    ssq = jnp.sum(ssq_ref[...], axis=0)[:, :1]
    return lax.rsqrt(ssq * (1.0 / x_ref.shape[1]) + EPS)


def _side_casts(src_refs, dst_refs):
    for src_ref, dst_ref in zip(src_refs, dst_refs):
        dst_ref[...] = src_ref[...].astype(BF16)


def _side_cast_specs(side, n_j, n_i):
    w_all, layer = side
    K, N = w_all.shape[1:]
    n_steps = n_j * n_i
    rc = next(r for r in range(16, K + 1, 16) if K % r == 0 and K // r <= n_steps)
    last = K // rc - 1
    return (pl.BlockSpec((None, rc, N), lambda j, i: (layer, jnp.minimum(j * n_i + i, last), 0)),
            pl.BlockSpec((rc, N), lambda j, i: (jnp.minimum(j * n_i + i, last), 0)),
            jax.ShapeDtypeStruct((K, N), BF16))


def _mm_norm_kernel(*refs, epilogue, transposed, cfg, n_side):
    x_ref, g_ref, w_ref, ssq_ref = refs[:4]
    wb_ref = refs[-1]
    o_ref = refs[-2 - n_side]
    extra = refs[4:len(refs) - 2 - 2 * n_side]
    _side_casts(refs[len(refs) - 2 - 2 * n_side:len(refs) - 2 - n_side], refs[len(refs) - 1 - n_side:-1])

    @pl.when(pl.program_id(1) == 0)
    def _():
        wb_ref[...] = (w_ref[...] * g_ref[...]).astype(BF16)

    r = _row_scale(x_ref, ssq_ref)
    _mm_norm_slab(slice(0, x_ref.shape[0]), x_ref, wb_ref, r, extra, o_ref,
                  epilogue=epilogue, transposed=transposed, cfg=cfg)


def _mm_norm_slab(rows, x_ref, wb_ref, r, extra, o_ref, *, epilogue, transposed, cfg):
    if transposed:
        acc = lax.dot_general(x_ref[rows, :], wb_ref[...], (((1,), (1,)), ((), ())),
                              preferred_element_type=F32) * r[rows]
    else:
        acc = jnp.dot(x_ref[rows, :], wb_ref[...], preferred_element_type=F32) * r[rows]
    if epilogue == "gelu":
        o_ref[rows, :] = _gelu_exact(acc).astype(o_ref.dtype)
    elif epilogue in ("rope_q", "rope_kt"):
        cos_ref, sin_ref = extra
        dk = cfg[0]
        half = dk // 2
        scale = 1.0 if epilogue == "rope_q" else dk ** -0.5
        cos = cos_ref[rows, :] * scale
        sin = sin_ref[rows, :] * scale
        for h in range(acc.shape[1] // dk):
            t1 = acc[:, h * dk:h * dk + half]
            t2 = acc[:, h * dk + half:(h + 1) * dk]
            r1, r2 = t1 * cos - t2 * sin, t2 * cos + t1 * sin
            if epilogue == "rope_q":
                o_ref[rows, h * dk:h * dk + half] = r1.astype(o_ref.dtype)
                o_ref[rows, h * dk + half:(h + 1) * dk] = r2.astype(o_ref.dtype)
            else:
                o_ref[h * dk:h * dk + half, rows] = r1.T.astype(o_ref.dtype)
                o_ref[h * dk + half:(h + 1) * dk, rows] = r2.T.astype(o_ref.dtype)
    elif epilogue in ("fox_q", "fox_k", "fox_v"):
        hd = cfg[0]
        heads = acc.shape[1] // hd
        lane = lax.broadcasted_iota(jnp.int32, (acc.shape[0], hd), 1)
        if epilogue == "fox_v":
            ones_col = jnp.where(lane == 0, 1.0, 0.0).astype(o_ref.dtype)
        else:
            gain_ref, c3_ref = extra
            gain = gain_ref[...] * cfg[1]
        for h in range(heads):
            t = acc[:, h * hd:(h + 1) * hd]
            if epilogue == "fox_v":
                data, aug = t, ones_col
            else:
                data = t * lax.rsqrt(jnp.mean(t * t, axis=-1, keepdims=True) + EPS) * gain
                hi, mid, lo = (c3_ref[rows, p * heads + h:p * heads + h + 1] for p in range(3))
                if epilogue == "fox_q":
                    aug = jnp.where(lane == 0, hi, jnp.where(lane == 1, mid, jnp.where(
                        lane == 2, lo, jnp.where(lane < 6, 1.0, 0.0))))
                else:
                    aug = jnp.where(lane < 3, 1.0, jnp.where(lane == 3, -hi, jnp.where(
                        lane == 4, -mid, jnp.where(lane == 5, -lo, 0.0))))
            o_ref[rows, 2 * h * hd:(2 * h + 1) * hd] = data.astype(o_ref.dtype)
            o_ref[rows, (2 * h + 1) * hd:(2 * h + 2) * hd] = aug.astype(o_ref.dtype)
    else:
        o_ref[rows, :] = acc.astype(o_ref.dtype)


def _ssq_spec(ssq, tm):
    return pl.BlockSpec((ssq.shape[0], tm, LANES), lambda j, i: (0, i, 0))


def _mm_norm(x, gain, w_all, layer, g_layer, n_out, *, col0=0, transposed=False, epilogue=None,
             extra=(), extra_specs=(), cfg=None, sides=(), out_dtype=BF16, tm=1024, tn=1024,
             name="mm_norm"):
    xb, ssq = x
    T, K = xb.shape
    n_j, n_i = n_out // tn, T // tm
    if transposed:
        g_spec = pl.BlockSpec((None, 1, K), lambda j, i: (g_layer, 0, 0))
        w_spec = pl.BlockSpec((None, tn, K), lambda j, i: (layer, col0 + j, 0))
        wb_shape = (tn, K)
    else:
        g_spec = pl.BlockSpec((None, K, 1), lambda j, i: (g_layer, 0, 0))
        w_spec = pl.BlockSpec((None, K, tn), lambda j, i: (layer, 0, col0 + j))
        wb_shape = (K, tn)
    in_specs = [pl.BlockSpec((tm, K), lambda j, i: (i, 0)), g_spec, w_spec, _ssq_spec(ssq, tm),
                *extra_specs]
    operands = [xb, gain, w_all, ssq, *extra]
    widen = 2 if epilogue in ("fox_q", "fox_k", "fox_v") else 1
    if epilogue == "rope_kt":
        out_specs = [pl.BlockSpec((widen * tn, tm), lambda j, i: (j, i))]
        out_shape = [jax.ShapeDtypeStruct((widen * n_out, T), out_dtype)]
    else:
        out_specs = [pl.BlockSpec((tm, widen * tn), lambda j, i: (i, j))]
        out_shape = [jax.ShapeDtypeStruct((T, widen * n_out), out_dtype)]
    for side in sides:
        src_spec, dst_spec, dst_shape = _side_cast_specs(side, n_j, n_i)
        in_specs.append(src_spec)
        operands.append(side[0])
        out_specs.append(dst_spec)
        out_shape.append(dst_shape)
    out = pl.pallas_call(
        functools.partial(_mm_norm_kernel, epilogue=epilogue, transposed=transposed, cfg=cfg,
                          n_side=len(sides)),
        grid=(n_j, n_i),
        in_specs=in_specs,
        out_specs=out_specs,
        out_shape=out_shape,
        scratch_shapes=[pltpu.VMEM(wb_shape, BF16)],
        compiler_params=_params(2),
        name=name,
    )(*operands)
    return out if sides else out[0]


def _mm_swiglu_kernel(x_ref, wg_ref, wu_ref, ssq_ref, side_src, o_ref, side_dst):
    _side_casts((side_src,), (side_dst,))
    r = _row_scale(x_ref, ssq_ref)
    tm = x_ref.shape[0]
    for r0 in range(0, tm, SWIGLU_ROWS):
        rows = slice(r0, r0 + SWIGLU_ROWS)
        x = x_ref[rows, :]
        gate = jnp.dot(x, wg_ref[...], preferred_element_type=F32) * r[rows]
        up = jnp.dot(x, wu_ref[...], preferred_element_type=F32) * r[rows]
        o_ref[rows, :] = (jax.nn.silu(gate) * up).astype(o_ref.dtype)


def _mm_swiglu(x, wg, wu, side, *, tm=2048, tn=512):
    xb, ssq = x
    T, K = xb.shape
    F = wg.shape[-1]
    n_j, n_i = F // tn, T // tm
    wspec = pl.BlockSpec((K, tn), lambda j, i: (0, j))
    src_spec, dst_spec, dst_shape = _side_cast_specs(side, n_j, n_i)
    return pl.pallas_call(
        _mm_swiglu_kernel,
        grid=(n_j, n_i),
        in_specs=[pl.BlockSpec((tm, K), lambda j, i: (i, 0)), wspec, wspec, _ssq_spec(ssq, tm),
                  src_spec],
        out_specs=[pl.BlockSpec((tm, tn), lambda j, i: (i, j)), dst_spec],
        out_shape=[jax.ShapeDtypeStruct((T, F), BF16), dst_shape],
        compiler_params=_params(2),
        name="mm_swiglu",
    )(xb, wg, wu, ssq, side[0])


def _row_ssq(x):
    return jnp.broadcast_to(jnp.sum(x * x, axis=-1, keepdims=True), (x.shape[0], LANES))


def _entry_kernel(x_ref, pos_ref, invf_ref, xb_ref, ssq_ref, cos_ref, sin_ref):
    x = x_ref[...]
    xb_ref[...] = x.astype(BF16)
    ssq_ref[...] = _row_ssq(x)
    ang = pos_ref[...].astype(F32) * invf_ref[...]
    cos_ref[...] = jnp.cos(ang)
    sin_ref[...] = jnp.sin(ang)


def _entry(x, pos_col, inv_freq, tm=1024):
    T, D = x.shape
    half = inv_freq.shape[-1]
    rows = pl.BlockSpec((tm, half), lambda i: (i, 0))
    table = jax.ShapeDtypeStruct((T, half), F32)
    xb, ssq, cos, sin = pl.pallas_call(
        _entry_kernel,
        grid=(T // tm,),
        in_specs=[pl.BlockSpec((tm, D), lambda i: (i, 0)),
                  pl.BlockSpec((tm, 1), lambda i: (i, 0)),
                  pl.BlockSpec((1, half), lambda i: (0, 0))],
        out_specs=[pl.BlockSpec((tm, D), lambda i: (i, 0)),
                   pl.BlockSpec((None, tm, LANES), lambda i: (0, i, 0)), rows, rows],
        out_shape=[jax.ShapeDtypeStruct((T, D), BF16), jax.ShapeDtypeStruct((1, T, LANES), F32),
                   table, table],
        compiler_params=_params(1),
        name="entry",
    )(x, pos_col, inv_freq)
    return (xb, ssq), cos, sin


def _mm_residual_kernel(a_ref, w_ref, r_ref, *refs, mode):
    if mode == "gain":
        gain_ref, refs = refs[0], refs[1:]
    o_ref = refs[0]
    x_new = r_ref[...] + jnp.dot(a_ref[...], w_ref[...], preferred_element_type=F32)
    o_ref[...] = x_new
    if mode != "last":
        ob_ref, ssq_ref = refs[1:]
        ob_ref[...] = (x_new * gain_ref[...] if mode == "gain" else x_new).astype(BF16)
        ssq_ref[...] = _row_ssq(x_new)


def _residual_tiles(K, N):
    return 512, (N if K <= 2048 else N // 2)


def _mm_residual(a, wb, res, *, gain=None, last=False):
    T, K = a.shape
    N = wb.shape[-1]
    tm, tn = _residual_tiles(K, N)
    tile = pl.BlockSpec((tm, tn), lambda j, i: (i, j))
    in_specs = [pl.BlockSpec((tm, K), lambda j, i: (i, 0)),
                pl.BlockSpec((K, tn), lambda j, i: (0, j)),
                tile]
    operands = [a, wb, res]
    mode = "last" if last else ("plain" if gain is None else "gain")
    if mode == "gain":
        g_all, g_layer = gain
        in_specs.append(pl.BlockSpec((None, 1, tn), lambda j, i: (g_layer, 0, j)))
        operands.append(g_all)
    out_specs = [tile, tile, pl.BlockSpec((None, tm, LANES), lambda j, i: (j, i, 0))]
    out_shape = [jax.ShapeDtypeStruct((T, N), F32), jax.ShapeDtypeStruct((T, N), BF16),
                 jax.ShapeDtypeStruct((N // tn, T, LANES), F32)]
    n_out = 1 if last else 3
    out = pl.pallas_call(
        functools.partial(_mm_residual_kernel, mode=mode),
        grid=(N // tn, T // tm),
        in_specs=in_specs,
        out_specs=out_specs[:n_out],
        out_shape=out_shape[:n_out],
        compiler_params=_params(2),
        name="mm_residual",
    )(*operands)
    return out[0], (None if last else (out[1], out[2]))


def _retention_kernel(q_ref, kt_ref, vg_ref, din_ref, qd_ref, kd_ref, cd_ref, gn_ref,
                      y_ref, state_ref, *, heads, dk, dv):
    @pl.when(pl.program_id(1) == 0)
    def _():
        state_ref[...] = jnp.zeros_like(state_ref)

    g_off = heads * dv
    rows = q_ref.shape[0]
    for h in range(heads):
        qb = q_ref[:, h * dk:(h + 1) * dk]
        kt = kt_ref[h * dk:(h + 1) * dk, :]
        v = vg_ref[:, h * dv:(h + 1) * dv]
        scores = jnp.dot(qb, kt, preferred_element_type=F32) * din_ref[h]
        state = state_ref[h]
        q_dec = qb * qd_ref[h]
        o = (jnp.dot(scores.astype(BF16), v, preferred_element_type=F32)
             + jnp.dot(q_dec, state.astype(BF16), preferred_element_type=F32))
        kt_dec = kt * kd_ref[h]
        state_ref[h] = state * cd_ref[h] + jnp.dot(kt_dec, v, preferred_element_type=F32)
        gain = gn_ref[:, h * dv:(h + 1) * dv]
        for r0 in range(0, rows, NORM_ROWS):
            strip = o[r0:r0 + NORM_ROWS]
            sc = strip - jnp.mean(strip, axis=-1, keepdims=True)
            var = jnp.mean(sc * sc, axis=-1, keepdims=True)
            g = vg_ref[r0:r0 + NORM_ROWS, g_off + h * dv:g_off + (h + 1) * dv].astype(F32)
            y_ref[r0:r0 + NORM_ROWS, h * dv:(h + 1) * dv] = (
                jax.nn.silu(g) * (sc * lax.rsqrt(var + EPS) * gain)).astype(y_ref.dtype)


def _retention(q, kt, vg, gn_all, layer, batch, seq):
    T = q.shape[0]
    H, C = RET_HEADS, RET_BLOCK
    dk = q.shape[1] // H
    dv = vg.shape[1] // (2 * H)
    nc = seq // C
    log_gamma = jnp.log1p(-jnp.exp2(-5.0 - jnp.arange(H, dtype=F32)))
    idx = jnp.arange(C, dtype=F32)
    dist = idx[:, None] - idx[None, :]
    decay_in = jnp.where(dist >= 0,
                         jnp.exp(jnp.maximum(dist, 0.0)[None] * log_gamma[:, None, None]), 0.0)
    q_decay = jnp.exp((idx + 1.0)[None, :] * log_gamma[:, None])[..., None]
    q_decay = jnp.broadcast_to(q_decay, (H, C, dk)).astype(BF16)
    k_decay = jnp.exp((C - 1.0 - idx)[None, :] * log_gamma[:, None])[:, None, :]
    k_decay = jnp.broadcast_to(k_decay, (H, dk, C)).astype(BF16)
    chunk_decay = jnp.exp(C * log_gamma)[:, None, None]

    def whole(shape):
        return pl.BlockSpec(shape, lambda b, c: (0,) * len(shape))

    return pl.pallas_call(
        functools.partial(_retention_kernel, heads=H, dk=dk, dv=dv),
        grid=(batch, nc),
        in_specs=[pl.BlockSpec((C, H * dk), lambda b, c: (b * nc + c, 0)),
                  pl.BlockSpec((H * dk, C), lambda b, c: (0, b * nc + c)),
                  pl.BlockSpec((C, 2 * H * dv), lambda b, c: (b * nc + c, 0)),
                  whole((H, C, C)), whole((H, C, dk)), whole((H, dk, C)), whole((H, 1, 1)),
                  pl.BlockSpec((None, 1, H * dv), lambda b, c: (layer, 0, 0))],
        out_specs=pl.BlockSpec((C, H * dv), lambda b, c: (b * nc + c, 0)),
        out_shape=jax.ShapeDtypeStruct((T, H * dv), BF16),
        scratch_shapes=[pltpu.VMEM((H, dk, dv), F32)],
        compiler_params=_params(2),
        name="retention",
    )(q, kt, vg, decay_in, q_decay, k_decay, chunk_decay, gn_all)


def _sgu_kernel(u_ref, v_ref, lng_ref, lnb_ref, ws_ref, bs_ref, y_ref, *, groups):
    C = ws_ref.shape[-1]
    gw = v_ref.shape[1] // groups
    row = lax.broadcasted_iota(jnp.int32, (C, C), 0)
    col = lax.broadcasted_iota(jnp.int32, (C, C), 1)
    w_causal = [jnp.where(col <= row, ws_ref[g], 0.0).astype(BF16) for g in range(groups)]
    for c0 in range(0, v_ref.shape[0], C):
        rows = slice(c0, c0 + C)
        v = v_ref[rows, :].astype(F32)
        vc = v - jnp.mean(v, axis=-1, keepdims=True)
        var = jnp.mean(vc * vc, axis=-1, keepdims=True)
        vn = (vc * lax.rsqrt(var + EPS) * lng_ref[...] + lnb_ref[...]).astype(BF16)
        for g in range(groups):
            cols = slice(g * gw, (g + 1) * gw)
            mixed = jnp.dot(w_causal[g], vn[:, cols], preferred_element_type=F32)
            mixed = mixed + bs_ref[:, g:g + 1]
            y_ref[rows, cols] = u_ref[rows, cols] * mixed.astype(BF16)


def _sgu(z, lng_all, lnb_all, ws_all, bs_t, layer):
    T, dffn = z.shape
    half = dffn // 2
    G, C = GMLP_GROUPS, CHUNK
    R = 4 * C
    vec = pl.BlockSpec((None, 1, half), lambda i: (layer, 0, 0))
    return pl.pallas_call(
        functools.partial(_sgu_kernel, groups=G),
        grid=(T // R,),
        in_specs=[pl.BlockSpec((R, half), lambda i: (i, 0)),
                  pl.BlockSpec((R, half), lambda i: (i, 1)),
                  vec, vec,
                  pl.BlockSpec((None, G, C, C), lambda i: (layer, 0, 0, 0)),
                  pl.BlockSpec((None, C, G), lambda i: (layer, 0, 0))],
        out_specs=pl.BlockSpec((R, half), lambda i: (i, 0)),
        out_shape=jax.ShapeDtypeStruct((T, half), BF16),
        compiler_params=_params(1),
        name="sgu",
    )(z, z, lng_all, lnb_all, ws_all, bs_t)


def _fox_prep_kernel(x_ref, g_ref, wf_ref, ssq_ref, bf_ref, c3_ref, wfb_ref, carry_ref, *, group):
    @pl.when(pl.program_id(1) == 0)
    def _():
        carry_ref[...] = jnp.zeros_like(carry_ref)
        wfb_ref[...] = (wf_ref[...] * g_ref[...]).astype(BF16)

    f_logit = lax.dot_general(x_ref[...], wfb_ref[...], (((1,), (1,)), ((), ())),
                              preferred_element_type=F32) * _row_scale(x_ref, ssq_ref)
    log_f = jax.nn.log_sigmoid(f_logit + bf_ref[...])
    tb, heads = log_f.shape
    row = lax.broadcasted_iota(jnp.int32, (tb, tb), 0)
    col = lax.broadcasted_iota(jnp.int32, (tb, tb), 1)
    tri = jnp.where(col <= row, 1.0, 0.0).astype(BF16)

    def split3(x):
        hi = x.astype(BF16)
        r1 = x - hi.astype(F32)
        mid = r1.astype(BF16)
        lo = (r1 - mid.astype(F32)).astype(BF16)
        return hi, mid, lo

    c = carry_ref[...] + sum(jnp.dot(tri, part, preferred_element_type=F32)
                             for part in split3(log_f))
    carry_ref[...] = c[tb - 1:tb, :]
    parts = [part.astype(F32) for part in split3(c * LOG2E)]
    pad = jnp.zeros((tb, LANES - 3 * group), F32)
    for g in range(heads // group):
        c3_ref[g] = jnp.concatenate(
            [part[:, g * group:(g + 1) * group] for part in parts] + [pad], axis=1)


def _fox_prep(x, gain_row, w_t, layer, g_layer, row0, bf_all, group, batch, seq, tb=512):
    xb, ssq = x
    T, K = xb.shape
    H = bf_all.shape[-1]
    nb = seq // tb
    return pl.pallas_call(
        functools.partial(_fox_prep_kernel, group=group),
        grid=(batch, nb),
        in_specs=[pl.BlockSpec((tb, K), lambda b, s: (b * nb + s, 0)),
                  pl.BlockSpec((None, 1, K), lambda b, s: (g_layer, 0, 0)),
                  pl.BlockSpec((None, H, K), lambda b, s: (layer, row0 // H, 0)),
                  pl.BlockSpec((ssq.shape[0], tb, LANES), lambda b, s: (0, b * nb + s, 0)),
                  pl.BlockSpec((None, 1, H), lambda b, s: (layer, 0, 0))],
        out_specs=pl.BlockSpec((H // group, tb, LANES), lambda b, s: (0, b * nb + s, 0)),
        out_shape=jax.ShapeDtypeStruct((H // group, T, LANES), F32),
        scratch_shapes=[pltpu.VMEM((H, K), BF16), pltpu.VMEM((1, H), F32)],
        compiler_params=_params(2),
        name="fox_prep",
    )(xb, gain_row, w_t, ssq, bf_all)


def _fox_attn_kernel(qi_tab, ki_tab, qa_ref, ka_ref, va_ref, g_ref, *refs, heads, hd, n_side):
    y_ref, (m_ref, acc_ref) = refs[n_side], refs[-2:]
    _side_casts(refs[:n_side], refs[n_side + 1:2 * n_side + 1])
    p = pl.program_id(1)
    qi = qi_tab[p]
    ki = ki_tab[p]

    @pl.when(ki == 0)
    def _():
        m_ref[...] = jnp.full_like(m_ref, -jnp.inf)
        acc_ref[...] = jnp.zeros_like(acc_ref)

    tq = qa_ref.shape[0]

    def update(h, r0, r1, n_keys, masked):
        wide = slice(2 * h * hd, 2 * (h + 1) * hd)
        s = lax.dot_general(qa_ref[r0:r1, wide], ka_ref[0:n_keys, wide], (((1,), (1,)), ((), ())),
                            preferred_element_type=F32)
        if masked:
            row = lax.broadcasted_iota(jnp.int32, s.shape, 0) + r0
            col = lax.broadcasted_iota(jnp.int32, s.shape, 1)
            s = jnp.where(col <= row, s, -jnp.inf)
        m_prev = m_ref[h, r0:r1]
        m_new = jnp.maximum(m_prev, jnp.max(s, axis=-1, keepdims=True))
        alpha = jnp.exp2(m_prev - m_new)
        pexp = jnp.exp2(s - m_new[:, :1]).astype(BF16)
        pv = jnp.dot(pexp, va_ref[0:n_keys, wide], preferred_element_type=F32)
        acc_ref[h, r0:r1] = jnp.concatenate([alpha, alpha], axis=1) * acc_ref[h, r0:r1] + pv
        m_ref[h, r0:r1] = m_new

    @pl.when(ki < qi)
    def _():
        for h in range(heads):
            update(h, 0, tq, tq, False)

    @pl.when(ki == qi)
    def _():
        for h in range(heads):
            update(h, 0, tq // 2, tq // 2, True)
            update(h, tq // 2, tq, tq, True)
        for h in range(heads):
            sl = slice(h * hd, (h + 1) * hd)
            acc = acc_ref[h]
            o = acc[:, :hd] / acc[:, hd:hd + 1]
            y_ref[:, sl] = (jax.nn.sigmoid(g_ref[:, sl].astype(F32)) * o).astype(y_ref.dtype)


def _fox_attention(qa, ka, va, gate, batch, seq, sides=(), tq=512):
    T, width = gate.shape
    H = FOX_HEADS
    hd = width // H
    nq = seq // tq
    pairs = [(qi, ki) for qi in range(nq) for ki in range(qi + 1)]
    qi_tab = jnp.asarray([p[0] for p in pairs], jnp.int32)
    ki_tab = jnp.asarray([p[1] for p in pairs], jnp.int32)

    def q_side(cols):
        return pl.BlockSpec((tq, cols), lambda b, p, qt, kt: (b * nq + qt[p], 0))

    def k_side(cols):
        return pl.BlockSpec((tq, cols), lambda b, p, qt, kt: (b * nq + kt[p], 0))

    in_specs = [q_side(2 * width), k_side(2 * width), k_side(2 * width), q_side(width)]
    out_specs = [q_side(width)]
    out_shape = [jax.ShapeDtypeStruct((T, width), BF16)]
    for side in sides:
        src_spec, dst_spec, dst_shape = _side_cast_specs(side, batch, len(pairs))
        in_specs.append(pl.BlockSpec(src_spec.block_shape,
                                     lambda b, p, qt, kt, f=src_spec.index_map: f(b, p)))
        out_specs.append(pl.BlockSpec(dst_spec.block_shape,
                                      lambda b, p, qt, kt, f=dst_spec.index_map: f(b, p)))
        out_shape.append(dst_shape)
    grid_spec = pltpu.PrefetchScalarGridSpec(
        num_scalar_prefetch=2,
        grid=(batch, len(pairs)),
        in_specs=in_specs,
        out_specs=out_specs,
        scratch_shapes=[pltpu.VMEM((H, tq, LANES), F32), pltpu.VMEM((H, tq, 2 * hd), F32)],
    )
    out = pl.pallas_call(
        functools.partial(_fox_attn_kernel, heads=H, hd=hd, n_side=len(sides)),
        grid_spec=grid_spec,
        out_shape=out_shape,
        compiler_params=_params(2),
        name="fox_attention",
    )(qi_tab, ki_tab, qa, ka, va, gate, *(side[0] for side in sides))
    return out if sides else out[0]


def kernel(x, positions, mix_norm_g, ffn_norm_g, ret_w_in, ret_gn_g, ret_w_out, gmlp_w_in, gmlp_ln_g, gmlp_ln_b, gmlp_w_s, gmlp_b_s, gmlp_w_out, fox_w_in, fox_b_f, fox_qn_g, fox_kn_g, fox_w_out, ffn_w_gate, ffn_w_up, ffn_w_down):
    B, S, D = x.shape
    T = B * S
    depth = mix_norm_g.shape[0]
    xf = x.reshape(T, D)

    def rows(a):
        return a.reshape(a.shape[0], 1, a.shape[1])

    mix_g = mix_norm_g[..., None]
    mix_g_row, ffn_g_row = rows(mix_norm_g), rows(ffn_norm_g)
    ret_gn, lng, lnb = rows(ret_gn_g), rows(gmlp_ln_g), rows(gmlp_ln_b)
    qn_g, kn_g, fox_bf = rows(fox_qn_g), rows(fox_kn_g), rows(fox_b_f)
    bs_t = jnp.swapaxes(gmlp_b_s, 1, 2)
    fox_w_t = jnp.swapaxes(fox_w_in, 1, 2)

    tm = tn = 1024
    ret_dk = ret_w_in.shape[-1] // (6 * RET_HEADS)
    ret_qk = 2 * RET_HEADS * ret_dk
    inv_freq = (ROPE_BASE ** (-jnp.arange(ret_dk // 2, dtype=F32) / (ret_dk // 2))).reshape(1, -1)
    xn, cos, sin = _entry(xf, positions.reshape(T, 1), inv_freq)
    rope_spec = pl.BlockSpec((tm, ret_dk // 2), lambda j, i: (i, 0))

    hd = fox_qn_g.shape[-1]
    fox_d = FOX_HEADS * hd
    for i in range(depth):
        kind, j = i % N_MIXERS, i // N_MIXERS
        if kind == 0:
            rope = dict(extra=(cos, sin), extra_specs=(rope_spec, rope_spec), cfg=(ret_dk,),
                        tm=tm, tn=tn)
            q = _mm_norm(xn, mix_g, ret_w_in, j, i, ret_qk // 2, epilogue="rope_q",
                         name="mm_ret_q", **rope)
            kt = _mm_norm(xn, mix_g, ret_w_in, j, i, ret_qk // 2, col0=ret_qk // 2 // tn,
                          epilogue="rope_kt", name="mm_ret_k", **rope)
            vg, w_out, w_gate, w_up = _mm_norm(
                xn, mix_g, ret_w_in, j, i, ret_w_in.shape[-1] - ret_qk, col0=ret_qk // tn,
                sides=[(ret_w_out, j), (ffn_w_gate, i), (ffn_w_up, i)], tm=tm, tn=tn,
                name="mm_ret_vg")
            y = _retention(q, kt, vg, ret_gn, j, B, S)
        elif kind == 1:
            z, w_out, w_gate, w_up = _mm_norm(
                xn, mix_g, gmlp_w_in, j, i, gmlp_w_in.shape[-1], epilogue="gelu",
                sides=[(gmlp_w_out, j), (ffn_w_gate, i), (ffn_w_up, i)], name="mm_gmlp_in")
            y = _sgu(z, lng, lnb, gmlp_w_s, bs_t, j)
        else:
            c3 = _fox_prep(xn, mix_g_row, fox_w_t, j, i, 4 * fox_d, fox_bf, tn // hd, B, S)
            gain_spec = pl.BlockSpec((None, 1, hd), lambda jj, ii: (j, 0, 0))
            c3_spec = pl.BlockSpec((None, tm, LANES), lambda jj, ii: (jj, ii, 0))
            qa = _mm_norm(
                xn, mix_g_row, fox_w_t, j, i, fox_d, transposed=True, epilogue="fox_q",
                extra=(qn_g, c3), extra_specs=(gain_spec, c3_spec), cfg=(hd, hd ** -0.5 * LOG2E),
                tm=tm, tn=tn, name="mm_fox_q")
            ka = _mm_norm(
                xn, mix_g_row, fox_w_t, j, i, fox_d, col0=fox_d // tn, transposed=True,
                epilogue="fox_k", extra=(kn_g, c3), extra_specs=(gain_spec, c3_spec), cfg=(hd, 1.0),
                tm=tm, tn=tn, name="mm_fox_k")
            va = _mm_norm(xn, mix_g_row, fox_w_t, j, i, fox_d, col0=2 * fox_d // tn,
                          transposed=True, epilogue="fox_v", cfg=(hd,), tm=tm, tn=tn,
                          name="mm_fox_v")
            gate, w_out = _mm_norm(xn, mix_g_row, fox_w_t, j, i, fox_d, col0=3 * fox_d // tn,
                                   transposed=True, sides=[(fox_w_out, j)], tm=tm, tn=tn,
                                   name="mm_fox_gate")
            y, w_gate, w_up = _fox_attention(qa, ka, va, gate, B, S,
                                             sides=[(ffn_w_gate, i), (ffn_w_up, i)])
        xf, xn = _mm_residual(y, w_out, xf, gain=(ffn_g_row, i))
        a, w_down = _mm_swiglu(xn, w_gate, w_up, (ffn_w_down, i))
        xf, xn = _mm_residual(a, w_down, xf, last=i == depth - 1)
    return xf.reshape(B, S, D)
```
